```python
import jax, jax.numpy as jnp
from jax import lax
import numpy as np

D_MODEL = 4096
BATCH = 4
SEQ = 4096
DEPTH = 1

GRID_W = 64
CTX_LEN = 256
EXPAND = 2
MIX_W = EXPAND * D_MODEL
LRU_W = MIX_W // 2
POOL_W = MIX_W - LRU_W
LRU_HEADS = 16
LRU_HEAD_DIM = LRU_W // LRU_HEADS
CONV_W = 4
CONV_LEFT = 1
LRU_C = 8.0
POOL_WINDOWS = (2, 4, 8, 16)
POOL_GROUPS = len(POOL_WINDOWS)
POOL_GROUP_DIM = POOL_W // POOL_GROUPS
EPS = 1e-6

kernel_name = 'hybrid_rglru_pool_dit_block'


def rmsnorm(x, g):
    xf = x.astype(jnp.float32)
    y = xf * lax.rsqrt(jnp.mean(xf * xf, axis=-1, keepdims=True) + EPS) * g.astype(jnp.float32)
    return y.astype(x.dtype)


def modulate(h, shift, scale):
    return h * (1.0 + scale) + shift


def depthwise_conv(u, w, b):
    L = u.shape[1]
    up = jnp.pad(u, ((0, 0), (CONV_LEFT, CONV_W - 1 - CONV_LEFT), (0, 0)))
    y = b
    for k in range(CONV_W):
        y = y + up[:, k:k + L] * w[k]
    return y


def rglru_coeffs(u, lam, w_r, b_r, w_i, b_i):
    uf = u.astype(jnp.float32)
    Bn, L, _ = uf.shape
    uh = uf.reshape(Bn, L, LRU_HEADS, LRU_HEAD_DIM)
    r = jax.nn.sigmoid(jnp.einsum('blhi,hij->blhj', uh, w_r.astype(jnp.float32)).reshape(Bn, L, LRU_W) + b_r.astype(jnp.float32))
    i = jax.nn.sigmoid(jnp.einsum('blhi,hij->blhj', uh, w_i.astype(jnp.float32)).reshape(Bn, L, LRU_W) + b_i.astype(jnp.float32))
    log_a = (-LRU_C * jax.nn.softplus(-lam.astype(jnp.float32))) * r
    a = jnp.exp(log_a)
    b = jnp.sqrt(-jnp.expm1(2.0 * log_a)) * (i * uf)
    return a, b


def linear_scan(a, b, h0, reverse, return_seq):
    aT = jnp.swapaxes(a, 0, 1)
    bT = jnp.swapaxes(b, 0, 1)

    def step(h, ab):
        at, bt = ab
        h = at * h + bt
        return h, (h if return_seq else None)

    h_last, hs = lax.scan(step, h0, (aT, bT), reverse=reverse)
    return (jnp.swapaxes(hs, 0, 1) if return_seq else None), h_last


def rglru_branch(xa_lat, xa_ctx, conv_w, conv_b, lam, w_r, b_r, w_i, b_i, ctx_out):
    u_lat = depthwise_conv(xa_lat, conv_w, conv_b)
    u_ctx = depthwise_conv(xa_ctx, conv_w, conv_b)
    ys_lat, ys_ctx = [], []
    for d, reverse in enumerate((False, True)):
        a_c, b_c = rglru_coeffs(u_ctx, lam[d], w_r[d], b_r[d], w_i[d], b_i[d])
        h0 = jnp.zeros((u_ctx.shape[0], LRU_W), jnp.float32)
        hs_c, h_c = linear_scan(a_c, b_c, h0, reverse, ctx_out)
        a_l, b_l = rglru_coeffs(u_lat, lam[d], w_r[d], b_r[d], w_i[d], b_i[d])
        hs_l, _ = linear_scan(a_l, b_l, h_c, reverse, True)
        ys_lat.append(hs_l)
        ys_ctx.append(hs_c)
    y_lat = ys_lat[0] + ys_lat[1]
    y_ctx = (ys_ctx[0] + ys_ctx[1]) if ctx_out else None
    return y_lat, y_ctx


def centred_mean(v, w):
    L = v.shape[-2]
    left = w // 2
    right = w - 1 - left
    S = jnp.concatenate([jnp.zeros_like(v[..., :1, :]), lax.cumsum(v, axis=v.ndim - 2)], axis=-2)
    t = np.arange(L)
    lo = np.maximum(t - left, 0)
    hi = np.minimum(t + right, L - 1) + 1
    sums = jnp.take(S, jnp.asarray(hi), axis=-2) - jnp.take(S, jnp.asarray(lo), axis=-2)
    cnt = jnp.asarray((hi - lo).astype(np.float32))[:, None]
    return sums / cnt


def pool_branch(u, w_pool, b_pool, scale):
    uf = u.astype(jnp.float32)
    grp = uf.reshape(uf.shape[:-1] + (POOL_GROUPS, POOL_GROUP_DIM))
    outs = [centred_mean(grp[..., g, :], w) - grp[..., g, :] for g, w in enumerate(POOL_WINDOWS)]
    z = jnp.stack(outs, axis=-2)
    y = jnp.einsum('...gi,gij->...gj', z, w_pool.astype(jnp.float32)).reshape(uf.shape) + b_pool
    return y * scale


def layer(x, ctx, c, c_ctx, w_ada, b_ada, g_norm, w_in, conv_w, conv_b, lam,
          w_r, b_r, w_i, b_i, w_pool, b_pool, pool_scale, w_out, last):
    D = D_MODEL
    Bn, L, _ = x.shape
    rows = L // GRID_W
    ctx_out = not last
    mod = jax.nn.silu(c) @ w_ada + b_ada
    shift, scale, gate = jnp.split(mod, 3, axis=-1)
    h = modulate(rmsnorm(x, g_norm), shift[:, None], scale[:, None])
    proj = h @ w_in
    xa, xb, ga, gb = jnp.split(proj, [LRU_W, MIX_W, MIX_W + LRU_W], axis=-1)
    if ctx_out:
        mod_c = jax.nn.silu(c_ctx) @ w_ada + b_ada
        shift_c, scale_c, gate_c = jnp.split(mod_c, 3, axis=-1)
        hc = modulate(rmsnorm(ctx, g_norm), shift_c, scale_c)
        proj_c = hc @ w_in
    else:
        mod_c = jax.nn.silu(c_ctx) @ w_ada[:, :2 * D] + b_ada[:2 * D]
        shift_c, scale_c = jnp.split(mod_c, 2, axis=-1)
        hc = modulate(rmsnorm(ctx, g_norm), shift_c, scale_c)
        proj_c = hc @ w_in[:, :LRU_W]
    xa_c = proj_c[..., :LRU_W]
    ya, ya_c = rglru_branch(xa, xa_c, conv_w, conv_b, lam, w_r, b_r, w_i, b_i, ctx_out)
    yb = pool_branch(xb.reshape(Bn, rows, GRID_W, POOL_W), w_pool, b_pool, pool_scale).reshape(Bn, L, POOL_W)
    mixed = jnp.concatenate([ya * jax.nn.silu(ga.astype(jnp.float32)),
                             yb * jax.nn.silu(gb.astype(jnp.float32))], axis=-1).astype(x.dtype)
    x_new = (x + gate[:, None] * (mixed @ w_out)).astype(x.dtype)
    if ctx_out:
        _, xb_c, ga_c, gb_c = jnp.split(proj_c, [LRU_W, MIX_W, MIX_W + LRU_W], axis=-1)
        yb_c = pool_branch(xb_c, w_pool, b_pool, pool_scale)
        mixed_c = jnp.concatenate([ya_c * jax.nn.silu(ga_c.astype(jnp.float32)),
                                   yb_c * jax.nn.silu(gb_c.astype(jnp.float32))], axis=-1).astype(ctx.dtype)
        ctx_new = (ctx + gate_c * (mixed_c @ w_out)).astype(ctx.dtype)
    else:
        ctx_new = None
    return x_new, ctx_new


def setup_inputs(seed: int = 0) -> dict:
    key = jax.random.key(seed)
    ks = jax.random.split(key, 20)
    D = D_MODEL
    f32 = jnp.float32
    x = jax.random.normal(ks[0], (BATCH, SEQ, D), f32)
    c = jax.random.normal(ks[1], (BATCH, D), f32)
    ctx = jax.random.normal(ks[2], (BATCH, CTX_LEN, D), f32)
    c_ctx = jax.random.normal(ks[3], (D,), f32)
    w_ada = jax.random.normal(ks[4], (DEPTH, D, 3 * D), f32) * (0.5 * D ** -0.5)
    b_ada = 0.01 * jax.random.normal(ks[5], (DEPTH, 3 * D), f32)
    g_norm = 1.0 + 0.05 * jax.random.normal(ks[6], (DEPTH, D), f32)
    w_in = jax.random.normal(ks[7], (DEPTH, D, 2 * MIX_W), f32) * D ** -0.5
    conv_w = jax.random.normal(ks[8], (DEPTH, CONV_W, LRU_W), f32) * CONV_W ** -0.5
    conv_b = 0.01 * jax.random.normal(ks[9], (DEPTH, LRU_W), f32)
    a_target = jax.random.uniform(ks[10], (DEPTH, 2, LRU_W), f32, minval=0.9, maxval=0.999) ** (1.0 / LRU_C)
    lru_lambda = jnp.log(a_target) - jnp.log1p(-a_target)
    w_rgate = jax.random.normal(ks[11], (DEPTH, 2, LRU_HEADS, LRU_HEAD_DIM, LRU_HEAD_DIM), f32) * LRU_HEAD_DIM ** -0.5
    b_rgate = 0.01 * jax.random.normal(ks[12], (DEPTH, 2, LRU_W), f32)
    w_igate = jax.random.normal(ks[13], (DEPTH, 2, LRU_HEADS, LRU_HEAD_DIM, LRU_HEAD_DIM), f32) * LRU_HEAD_DIM ** -0.5
    b_igate = 0.01 * jax.random.normal(ks[14], (DEPTH, 2, LRU_W), f32)
    w_pool = jax.random.normal(ks[15], (DEPTH, POOL_GROUPS, POOL_GROUP_DIM, POOL_GROUP_DIM), f32) * POOL_GROUP_DIM ** -0.5
    b_pool = 0.01 * jax.random.normal(ks[16], (DEPTH, POOL_W), f32)
    pool_scale = 1.0 + 0.1 * jax.random.normal(ks[17], (DEPTH, POOL_W), f32)
    w_out = jax.random.normal(ks[18], (DEPTH, MIX_W, D), f32) * MIX_W ** -0.5
    g_final = 1.0 + 0.05 * jax.random.normal(ks[19], (D,), f32)
    return {'x': x, 'c': c, 'ctx': ctx, 'c_ctx': c_ctx, 'w_ada': w_ada, 'b_ada': b_ada,
            'g_norm': g_norm, 'w_in': w_in, 'conv_w': conv_w, 'conv_b': conv_b,
            'lru_lambda': lru_lambda, 'w_rgate': w_rgate, 'b_rgate': b_rgate,
            'w_igate': w_igate, 'b_igate': b_igate, 'w_pool': w_pool, 'b_pool': b_pool,
            'pool_scale': pool_scale, 'w_out': w_out, 'g_final': g_final}


def reference(x, c, ctx, c_ctx, w_ada, b_ada, g_norm, w_in, conv_w, conv_b, lru_lambda,
              w_rgate, b_rgate, w_igate, b_igate, w_pool, b_pool, pool_scale, w_out, g_final):
    for l in range(DEPTH):
        x, ctx = layer(x, ctx, c, c_ctx, w_ada[l], b_ada[l], g_norm[l], w_in[l], conv_w[l], conv_b[l],
                       lru_lambda[l], w_rgate[l], b_rgate[l], w_igate[l], b_igate[l],
                       w_pool[l], b_pool[l], pool_scale[l], w_out[l], last=(l == DEPTH - 1))
    return rmsnorm(x, g_final)
```

```python
import functools

import jax
import jax.numpy as jnp
import numpy as np
from jax import lax
from jax.experimental import pallas as pl
from jax.experimental.pallas import tpu as pltpu

D_MODEL = 4096
GRID_W = 64
MIX_W = 2 * D_MODEL
LRU_W = MIX_W // 2
POOL_W = MIX_W - LRU_W
LRU_HEADS = 16
LRU_HEAD_DIM = LRU_W // LRU_HEADS
CONV_W = 4
CONV_LEFT = 1
LRU_C = 8.0
POOL_WINDOWS = (2, 4, 8, 16)
POOL_GROUP_DIM = POOL_W // len(POOL_WINDOWS)
EPS = 1e-6

LANES = 128
VMEM_LIMIT_BYTES = 56 * 1024 * 1024

BF16 = jnp.bfloat16
F32 = jnp.float32


def _silu(v):
    return v * jax.nn.sigmoid(v)


def _adaln_kernel(c_ref, w_ref, b_ref, o_ref):
    s = _silu(c_ref[...]).astype(BF16)
    o_ref[...] = jnp.dot(s, w_ref[...].astype(BF16), preferred_element_type=F32) + b_ref[...]


def _adaln(cc, w_ada, b_ada, tn=512):
    rows, d = cc.shape
    n = w_ada.shape[1]
    return pl.pallas_call(
        _adaln_kernel,
        grid=(n // tn,),
        in_specs=[
            pl.BlockSpec((rows, d), lambda j: (0, 0)),
            pl.BlockSpec((d, tn), lambda j: (0, j)),
            pl.BlockSpec((1, tn), lambda j: (0, j)),
        ],
        out_specs=pl.BlockSpec((rows, tn), lambda j: (0, j)),
        out_shape=jax.ShapeDtypeStruct((rows, n), F32),
        compiler_params=pltpu.CompilerParams(
            dimension_semantics=("parallel",), vmem_limit_bytes=VMEM_LIMIT_BYTES),
        name="adaln",
    )(cc, w_ada, b_ada.reshape(1, n))


def _inproj_kernel(x_ref, g_ref, shift_ref, scale_ref, w_ref, o_ref, h_ref):
    @pl.when(pl.program_id(1) == 0)
    def _():
        xf = x_ref[...]
        ms = jnp.mean(xf * xf, axis=-1, keepdims=True)
        y = xf * lax.rsqrt(ms + EPS) * g_ref[...]
        h_ref[...] = (y * (1.0 + scale_ref[0]) + shift_ref[0]).astype(BF16)

    o_ref[...] = jnp.dot(h_ref[...], w_ref[...], preferred_element_type=F32).astype(o_ref.dtype)


def _in_proj(x2d, g_norm, shift, scale, w_bf16, rows_per_mod, tm=512, tn=1024):
    m, d = x2d.shape
    n = w_bf16.shape[1]
    tiles_per_mod = rows_per_mod // tm
    return pl.pallas_call(
        _inproj_kernel,
        grid=(m // tm, n // tn),
        in_specs=[
            pl.BlockSpec((tm, d), lambda i, j: (i, 0)),
            pl.BlockSpec((1, d), lambda i, j: (0, 0)),
            pl.BlockSpec((1, 1, d), lambda i, j: (i // tiles_per_mod, 0, 0)),
            pl.BlockSpec((1, 1, d), lambda i, j: (i // tiles_per_mod, 0, 0)),
            pl.BlockSpec((d, tn), lambda i, j: (0, j)),
        ],
        out_specs=pl.BlockSpec((tm, tn), lambda i, j: (i, j)),
        out_shape=jax.ShapeDtypeStruct((m, n), BF16),
        scratch_shapes=[pltpu.VMEM((tm, d), BF16)],
        compiler_params=pltpu.CompilerParams(
            dimension_semantics=("parallel", "arbitrary"), vmem_limit_bytes=VMEM_LIMIT_BYTES),
        name="in_proj",
    )(x2d, g_norm.reshape(1, d), shift, scale, w_bf16)


LRU_TILE = 512
LRU_GROUPS = LRU_TILE // LANES
LRU_CHUNK = 256
HALO = 16
CHUNK_PITCH = LRU_CHUNK + 8


def _lru_kernel(xa_ref, ga_ref, xc_ref, cw_ref, cb_ref, lam_ref, wg_ref, br_ref, bi_ref,
                o_ref, af_ref, bf_ref, ar_ref, br_s_ref, y_ref, *, seq_len, ctx_len):
    n_chunks = seq_len // LRU_CHUNK
    y_pitch = seq_len + 8
    heads = LRU_TILE // LRU_HEAD_DIM
    coef_refs = ((af_ref, bf_ref), (ar_ref, br_s_ref))

    cw = cw_ref[...]
    cb = cb_ref[...]
    decay = -LRU_C * jax.nn.softplus(-lam_ref[...])

    def conv(x_pad, rows):
        n = rows + 2 * HALO
        u = cb + x_pad[HALO:HALO + rows] * cw[CONV_LEFT:CONV_LEFT + 1]
        for k in range(CONV_W):
            off = k - CONV_LEFT
            if off == 0:
                continue
            u = u + pltpu.roll(x_pad, (-off) % n, axis=0)[HALO:HALO + rows] * cw[k:k + 1]
        return u

    def coefficients(u, d, rows):
        a_ref, b_ref = coef_refs[d]
        for hh in range(heads):
            cols = slice(hh * LRU_HEAD_DIM, (hh + 1) * LRU_HEAD_DIM)
            uh = u[:, cols]
            gates = jnp.dot(uh.astype(BF16), wg_ref[d, hh], preferred_element_type=F32)
            r = jax.nn.sigmoid(gates[:, :LRU_HEAD_DIM] + br_ref[d:d + 1, cols])
            i = jax.nn.sigmoid(gates[:, LRU_HEAD_DIM:] + bi_ref[d:d + 1, cols])
            log_a = decay[d:d + 1, cols] * r
            a = jnp.exp(log_a)
            t = jnp.tanh(log_a)
            b = jnp.sqrt(-2.0 * t / (1.0 - t)) * (i * uh)
            for gg in range(LRU_HEAD_DIM // LANES):
                g = hh * (LRU_HEAD_DIM // LANES) + gg
                lanes = slice(gg * LANES, (gg + 1) * LANES)
                a_ref[pl.ds(g * CHUNK_PITCH, rows), :] = a[:, lanes]
                b_ref[pl.ds(g * CHUNK_PITCH, rows), :] = b[:, lanes]

    def latent_chunk(k):
        s = pl.multiple_of(k * LRU_CHUNK, LRU_CHUNK)
        s_lo = pl.multiple_of(jnp.maximum(s - HALO, 0), HALO)
        s_hi = pl.multiple_of(jnp.minimum(s + LRU_CHUNK, seq_len - HALO), HALO)
        lo = xa_ref[0, pl.ds(s_lo, HALO), :].astype(F32)
        hi = xa_ref[0, pl.ds(s_hi, HALO), :].astype(F32)
        lo = jnp.where(k > 0, lo, 0.0)
        hi = jnp.where(k < n_chunks - 1, hi, 0.0)
        mid = xa_ref[0, pl.ds(s, LRU_CHUNK), :].astype(F32)
        return conv(jnp.concatenate([lo, mid, hi], axis=0), LRU_CHUNK)

    def load_step(refs, j):
        a_ref, b_ref = refs
        return (a_ref[pl.ds(j, LRU_GROUPS, stride=CHUNK_PITCH), :],
                b_ref[pl.ds(j, LRU_GROUPS, stride=CHUNK_PITCH), :])

    zpad = jnp.zeros((HALO, LRU_TILE), F32)
    u_ctx = conv(jnp.concatenate([zpad, xc_ref[0].astype(F32), zpad], axis=0), ctx_len)
    coefficients(u_ctx, 0, ctx_len)
    coefficients(u_ctx, 1, ctx_len)

    def ctx_body(j, carry):
        hf, hr = carry
        a, b = load_step(coef_refs[0], j)
        hf = a * hf + b
        a, b = load_step(coef_refs[1], ctx_len - 1 - j)
        hr = a * hr + b
        return hf, hr

    h0 = jnp.zeros((LRU_GROUPS, LANES), F32)
    hf, hr = lax.fori_loop(0, ctx_len, ctx_body, (h0, h0), unroll=8)

    def scan_pair(k, carry, accumulate):
        kr = n_chunks - 1 - k
        coefficients(latent_chunk(k), 0, LRU_CHUNK)
        coefficients(latent_chunk(kr), 1, LRU_CHUNK)
        sf = k * LRU_CHUNK
        sr = kr * LRU_CHUNK

        def body(j, c):
            hf, hr = c
            a, b = load_step(coef_refs[0], j)
            hf = a * hf + b
            jr = LRU_CHUNK - 1 - j
            a, b = load_step(coef_refs[1], jr)
            hr = a * hr + b
            rows_f = pl.ds(sf + j, LRU_GROUPS, stride=y_pitch)
            rows_r = pl.ds(sr + jr, LRU_GROUPS, stride=y_pitch)
            if accumulate:
                y_ref[rows_f, :] = y_ref[rows_f, :] + hf
                y_ref[rows_r, :] = y_ref[rows_r, :] + hr
            else:
                y_ref[rows_f, :] = hf
                y_ref[rows_r, :] = hr
            return hf, hr

        return lax.fori_loop(0, LRU_CHUNK, body, carry, unroll=8)

    half = n_chunks // 2
    carry = lax.fori_loop(0, half, functools.partial(scan_pair, accumulate=False), (hf, hr))
    lax.fori_loop(half, n_chunks, functools.partial(scan_pair, accumulate=True), carry)

    def gate_body(k, _):
        s = pl.multiple_of(k * LRU_CHUNK, LRU_CHUNK)
        y = jnp.concatenate(
            [y_ref[pl.ds(g * y_pitch + s, LRU_CHUNK), :] for g in range(LRU_GROUPS)], axis=1)
        gate = ga_ref[0, pl.ds(s, LRU_CHUNK), :].astype(F32)
        o_ref[0, pl.ds(s, LRU_CHUNK), :] = (y * _silu(gate)).astype(o_ref.dtype)
        return 0

    lax.fori_loop(0, n_chunks, gate_body, 0)


def _lru(proj, proj_c, conv_w, conv_b, lam, wg, b_r, b_i):
    bsz, seq_len, _ = proj.shape
    ctx_len = proj_c.shape[1]
    assert seq_len % (2 * LRU_CHUNK) == 0 and ctx_len <= LRU_CHUNK and ctx_len % HALO == 0
    n_tiles = LRU_W // LRU_TILE
    heads = LRU_TILE // LRU_HEAD_DIM
    gate_block0 = (MIX_W) // LRU_TILE
    slab = pltpu.VMEM((LRU_GROUPS * CHUNK_PITCH, LANES), F32)
    kern = functools.partial(_lru_kernel, seq_len=seq_len, ctx_len=ctx_len)
    return pl.pallas_call(
        kern,
        grid=(bsz, n_tiles),
        in_specs=[
            pl.BlockSpec((1, seq_len, LRU_TILE), lambda b, t: (b, 0, t)),
            pl.BlockSpec((1, seq_len, LRU_TILE), lambda b, t: (b, 0, gate_block0 + t)),
            pl.BlockSpec((1, ctx_len, LRU_TILE), lambda b, t: (b, 0, t)),
            pl.BlockSpec((CONV_W, LRU_TILE), lambda b, t: (0, t)),
            pl.BlockSpec((1, LRU_TILE), lambda b, t: (0, t)),
            pl.BlockSpec((2, LRU_TILE), lambda b, t: (0, t)),
            pl.BlockSpec((2, heads, LRU_HEAD_DIM, 2 * LRU_HEAD_DIM), lambda b, t: (0, t, 0, 0)),
            pl.BlockSpec((2, LRU_TILE), lambda b, t: (0, t)),
            pl.BlockSpec((2, LRU_TILE), lambda b, t: (0, t)),
        ],
        out_specs=pl.BlockSpec((1, seq_len, LRU_TILE), lambda b, t: (b, 0, t)),
        out_shape=jax.ShapeDtypeStruct((bsz, seq_len, LRU_W), BF16),
        scratch_shapes=[slab, slab, slab, slab,
                        pltpu.VMEM((LRU_GROUPS * (seq_len + 8), LANES), F32)],
        compiler_params=pltpu.CompilerParams(
            dimension_semantics=("parallel", "parallel"), vmem_limit_bytes=VMEM_LIMIT_BYTES),
        name="lru",
    )(proj, proj, proj_c, conv_w, conv_b.reshape(1, LRU_W), lam, wg, b_r, b_i)


POOL_ROWS = 512
POOL_BAND = 256


def _pool_operators():
    t = np.arange(POOL_BAND)
    seg, pos = t // GRID_W, t % GRID_W
    wins, inv = [], []
    for w in POOL_WINDOWS:
        left = w // 2
        right = w - 1 - left
        lo = np.maximum(pos - left, 0)
        hi = np.minimum(pos + right, GRID_W - 1)
        member = ((seg[:, None] == seg[None, :])
                  & (pos[None, :] >= lo[:, None]) & (pos[None, :] <= hi[:, None]))
        wins.append(member.astype(np.float32))
        inv.append(np.broadcast_to((1.0 / (hi - lo + 1).astype(np.float32))[:, None],
                                   (POOL_BAND, LANES)))
    return jnp.asarray(np.stack(wins), BF16), jnp.asarray(np.stack(inv), F32)


def _pool_kernel(xb_ref, gb_ref, win_ref, inv_ref, wp_ref, bp_ref, sc_ref, o_ref):
    inv = jnp.tile(inv_ref[0], (1, POOL_GROUP_DIM // LANES))
    zs = []
    for sub in range(POOL_ROWS // POOL_BAND):
        xs = xb_ref[0, sub * POOL_BAND:(sub + 1) * POOL_BAND, :]
        sums = jnp.dot(win_ref[0], xs, preferred_element_type=F32)
        zs.append((sums * inv - xs.astype(F32)).astype(BF16))
    z = jnp.concatenate(zs, axis=0)
    y = (jnp.dot(z, wp_ref[0], preferred_element_type=F32) + bp_ref[...]) * sc_ref[...]
    o_ref[0] = (y * _silu(gb_ref[0].astype(F32))).astype(o_ref.dtype)


def _pool(proj, wp_bf16, b_pool, pool_scale):
    bsz, seq_len, _ = proj.shape
    assert seq_len % POOL_ROWS == 0 and POOL_BAND % GRID_W == 0
    groups = len(POOL_WINDOWS)
    xb_block0 = LRU_W // POOL_GROUP_DIM
    gb_block0 = (MIX_W + LRU_W) // POOL_GROUP_DIM
    win, inv = _pool_operators()
    return pl.pallas_call(
        _pool_kernel,
        grid=(groups, bsz, seq_len // POOL_ROWS),
        in_specs=[
            pl.BlockSpec((1, POOL_ROWS, POOL_GROUP_DIM), lambda g, b, l: (b, l, xb_block0 + g)),
            pl.BlockSpec((1, POOL_ROWS, POOL_GROUP_DIM), lambda g, b, l: (b, l, gb_block0 + g)),
            pl.BlockSpec((1, POOL_BAND, POOL_BAND), lambda g, b, l: (g, 0, 0)),
            pl.BlockSpec((1, POOL_BAND, LANES), lambda g, b, l: (g, 0, 0)),
            pl.BlockSpec((1, POOL_GROUP_DIM, POOL_GROUP_DIM), lambda g, b, l: (g, 0, 0)),
            pl.BlockSpec((1, POOL_GROUP_DIM), lambda g, b, l: (0, g)),
            pl.BlockSpec((1, POOL_GROUP_DIM), lambda g, b, l: (0, g)),
        ],
        out_specs=pl.BlockSpec((1, POOL_ROWS, POOL_GROUP_DIM), lambda g, b, l: (b, l, g)),
        out_shape=jax.ShapeDtypeStruct((bsz, seq_len, POOL_W), BF16),
        compiler_params=pltpu.CompilerParams(
            dimension_semantics=("parallel", "parallel", "parallel"),
            vmem_limit_bytes=VMEM_LIMIT_BYTES),
        name="pool",
    )(proj, proj, win, inv, wp_bf16, b_pool.reshape(1, POOL_W), pool_scale.reshape(1, POOL_W))


def _outproj_kernel(ma_ref, mb_ref, w_ref, x_ref, gate_ref, gf_ref, o_ref, *, k_half):
    k = pl.program_id(1)

    @pl.when(k == 0)
    def _():
        o_ref[...] = jnp.zeros_like(o_ref)

    @pl.when(k < k_half)
    def _():
        o_ref[...] += jnp.dot(ma_ref[...], w_ref[...], preferred_element_type=F32)

    @pl.when(k >= k_half)
    def _():
        o_ref[...] += jnp.dot(mb_ref[...], w_ref[...], preferred_element_type=F32)

    @pl.when(k == 2 * k_half - 1)
    def _():
        xn = x_ref[...] + gate_ref[0] * o_ref[...]
        ms = jnp.mean(xn * xn, axis=-1, keepdims=True)
        o_ref[...] = xn * lax.rsqrt(ms + EPS) * gf_ref[...]


def _out_proj(ma, mb, w_bf16, x2d, gate, g_final, rows_per_mod, tm=512, tk=256):
    m, d = x2d.shape
    k_half = ma.shape[1] // tk
    tiles_per_mod = rows_per_mod // tm
    kern = functools.partial(_outproj_kernel, k_half=k_half)
    return pl.pallas_call(
        kern,
        grid=(m // tm, 2 * k_half),
        in_specs=[
            pl.BlockSpec((tm, tk), lambda i, k: (i, jnp.minimum(k, k_half - 1))),
            pl.BlockSpec((tm, tk), lambda i, k: (i, jnp.maximum(k - k_half, 0))),
            pl.BlockSpec((tk, d), lambda i, k: (k, 0)),
            pl.BlockSpec((tm, d), lambda i, k: (i, 0)),
            pl.BlockSpec((1, 1, d), lambda i, k: (i // tiles_per_mod, 0, 0)),
            pl.BlockSpec((1, d), lambda i, k: (0, 0)),
        ],
        out_specs=pl.BlockSpec((tm, d), lambda i, k: (i, 0)),
        out_shape=jax.ShapeDtypeStruct((m, d), F32),
        compiler_params=pltpu.CompilerParams(
            dimension_semantics=("parallel", "arbitrary"), vmem_limit_bytes=VMEM_LIMIT_BYTES),
        name="out_proj",
    )(ma, mb, w_bf16, x2d, gate, g_final.reshape(1, d))


def kernel(x, c, ctx, c_ctx, w_ada, b_ada, g_norm, w_in, conv_w, conv_b, lru_lambda,
           w_rgate, b_rgate, w_igate, b_igate, w_pool, b_pool, pool_scale, w_out, g_final):
    bsz, seq_len, d = x.shape
    ctx_len = ctx.shape[1]
    depth = w_ada.shape[0]
    assert depth == 1 and d == D_MODEL
    l = 0

    cc = jnp.zeros((8, d), F32).at[:bsz].set(c).at[bsz].set(c_ctx)
    mod = _adaln(cc, w_ada[l], b_ada[l])
    shift = mod[:, None, :d]
    scale = mod[:, None, d:2 * d]
    gate = mod[:, None, 2 * d:]

    w_in_b = w_in[l].astype(BF16)
    x2d = x.reshape(bsz * seq_len, d)
    proj = _in_proj(x2d, g_norm[l], shift, scale, w_in_b, seq_len).reshape(bsz, seq_len, 2 * MIX_W)
    proj_c = _in_proj(ctx.reshape(bsz * ctx_len, d), g_norm[l], shift[bsz:bsz + 1],
                      scale[bsz:bsz + 1], w_in_b[:, :LRU_W], bsz * ctx_len,
                      ).reshape(bsz, ctx_len, LRU_W)

    wg = jnp.concatenate([w_rgate[l], w_igate[l]], axis=-1).astype(BF16)
    mixed_a = _lru(proj, proj_c, conv_w[l], conv_b[l], lru_lambda[l], wg, b_rgate[l], b_igate[l])
    mixed_b = _pool(proj, w_pool[l].astype(BF16), b_pool[l], pool_scale[l])

    out = _out_proj(mixed_a.reshape(bsz * seq_len, LRU_W), mixed_b.reshape(bsz * seq_len, POOL_W),
                    w_out[l].astype(BF16), x2d, gate, g_final, seq_len)
    return out.reshape(bsz, seq_len, d)
```

```python
import functools

import jax
import jax.numpy as jnp
import numpy as np
from jax import lax
from jax.experimental import pallas as pl
from jax.experimental.pallas import tpu as pltpu

D_MODEL = 4096
GRID_W = 64
MIX_W = 2 * D_MODEL
LRU_W = MIX_W // 2
POOL_W = MIX_W - LRU_W
LRU_HEADS = 16
LRU_HEAD_DIM = LRU_W // LRU_HEADS
CONV_W = 4
CONV_LEFT = 1
LRU_C = 8.0
POOL_WINDOWS = (2, 4, 8, 16)
POOL_GROUP_DIM = POOL_W // len(POOL_WINDOWS)
EPS = 1e-6

LANES = 128
VMEM_LIMIT_BYTES = 56 * 1024 * 1024

BF16 = jnp.bfloat16
F32 = jnp.float32


def _sigmoid(v):
    return 0.5 * jnp.tanh(0.5 * v) + 0.5


def _silu(v):
    return v * _sigmoid(v)


def _adaln_kernel(c_ref, w_ref, b_ref, o_ref):
    s = _silu(c_ref[...]).astype(BF16)
    o_ref[...] = jnp.dot(s, w_ref[...].astype(BF16), preferred_element_type=F32) + b_ref[...]


def _adaln(cc, w_ada, b_ada, tn=512):
    rows, d = cc.shape
    n = w_ada.shape[1]
    return pl.pallas_call(
        _adaln_kernel,
        grid=(n // tn,),
        in_specs=[
            pl.BlockSpec((rows, d), lambda j: (0, 0)),
            pl.BlockSpec((d, tn), lambda j: (0, j)),
            pl.BlockSpec((1, tn), lambda j: (0, j)),
        ],
        out_specs=pl.BlockSpec((rows, tn), lambda j: (0, j)),
        out_shape=jax.ShapeDtypeStruct((rows, n), F32),
        compiler_params=pltpu.CompilerParams(
            dimension_semantics=("parallel",), vmem_limit_bytes=VMEM_LIMIT_BYTES),
        name="adaln",
    )(cc, w_ada, b_ada.reshape(1, n))


def _inproj_kernel(x_ref, g_ref, shift_ref, scale_ref, w_ref, o_ref, h_ref, *, n_sub):
    j = pl.program_id(1)
    sub_rows = x_ref.shape[0]

    @pl.when(j < n_sub)
    def _():
        xf = x_ref[...]
        ms = jnp.mean(xf * xf, axis=-1, keepdims=True)
        y = xf * lax.rsqrt(ms + EPS) * g_ref[...]
        h = (y * (1.0 + scale_ref[0]) + shift_ref[0]).astype(BF16)
        h_ref[pl.ds(pl.multiple_of(j * sub_rows, sub_rows), sub_rows), :] = h

    @pl.when(j >= n_sub)
    def _():
        o_ref[...] = jnp.dot(h_ref[...], w_ref[...], preferred_element_type=F32).astype(o_ref.dtype)


def _in_proj(x2d, g_norm, shift, scale, w_bf16, rows_per_mod, n_cols, tm=1024, tn=1024,
             sub_rows=512):
    m, d = x2d.shape
    n_sub = tm // sub_rows
    tiles_per_mod = rows_per_mod // tm
    kern = functools.partial(_inproj_kernel, n_sub=n_sub)
    return pl.pallas_call(
        kern,
        grid=(m // tm, n_sub + n_cols // tn),
        in_specs=[
            pl.BlockSpec((sub_rows, d), lambda i, j: (i * n_sub + jnp.minimum(j, n_sub - 1), 0)),
            pl.BlockSpec((1, d), lambda i, j: (0, 0)),
            pl.BlockSpec((1, 1, d), lambda i, j: (i // tiles_per_mod, 0, 0)),
            pl.BlockSpec((1, 1, d), lambda i, j: (i // tiles_per_mod, 0, 0)),
            pl.BlockSpec((d, tn), lambda i, j: (0, jnp.maximum(j - n_sub, 0))),
        ],
        out_specs=pl.BlockSpec((tm, tn), lambda i, j: (i, jnp.maximum(j - n_sub, 0))),
        out_shape=jax.ShapeDtypeStruct((m, n_cols), BF16),
        scratch_shapes=[pltpu.VMEM((tm, d), BF16)],
        compiler_params=pltpu.CompilerParams(
            dimension_semantics=("parallel", "arbitrary"), vmem_limit_bytes=VMEM_LIMIT_BYTES),
        name="in_proj",
    )(x2d, g_norm.reshape(1, d), shift, scale, w_bf16)


LRU_TILE = 512
LRU_GROUPS = LRU_TILE // LANES
LRU_CHUNK = 256
HALO = 16
CHUNK_PITCH = LRU_CHUNK + 8


def _lru_kernel(xa_ref, ga_ref, xc_ref, cw_ref, cb_ref, lam_ref, wg_ref, br_ref, bi_ref,
                o_ref, af_ref, bf_ref, ar_ref, br_s_ref, hf_ref, hr_ref, y_ref, u_ref,
                *, seq_len, ctx_len):
    n_chunks = seq_len // LRU_CHUNK
    y_pitch = seq_len + 8
    heads = LRU_TILE // LRU_HEAD_DIM
    coef_refs = ((af_ref, bf_ref), (ar_ref, br_s_ref))

    cw = cw_ref[...]
    cb = cb_ref[...]
    half_decay = (-0.5 * LRU_C) * jax.nn.softplus(-lam_ref[...])
    half_br = 0.5 * br_ref[...]
    half_bi = 0.5 * bi_ref[...]

    def conv(x_pad, rows):
        n = rows + 2 * HALO
        u = cb + x_pad[HALO:HALO + rows] * cw[CONV_LEFT:CONV_LEFT + 1]
        for k in range(CONV_W):
            off = k - CONV_LEFT
            if off == 0:
                continue
            u = u + pltpu.roll(x_pad, (-off) % n, axis=0)[HALO:HALO + rows] * cw[k:k + 1]
        return u

    def coefficients(u, d, rows):
        a_ref, b_ref = coef_refs[d]
        for hh in range(heads):
            cols = slice(hh * LRU_HEAD_DIM, (hh + 1) * LRU_HEAD_DIM)
            uh = u[:, cols]
            gates = jnp.dot(uh.astype(BF16), wg_ref[d, hh], preferred_element_type=F32)
            tr = jnp.tanh(gates[:, :LRU_HEAD_DIM] + half_br[d:d + 1, cols])
            ti = jnp.tanh(gates[:, LRU_HEAD_DIM:] + half_bi[d:d + 1, cols])
            i = 0.5 * ti + 0.5
            log_a = half_decay[d:d + 1, cols] * tr + half_decay[d:d + 1, cols]
            a = jnp.exp(log_a)
            t = jnp.tanh(log_a)
            p = -2.0 * t
            q = 1.0 - t
            root = jnp.where(p > 0.0, p * lax.rsqrt(p * q), 0.0)
            b = root * (i * uh)
            for gg in range(LRU_HEAD_DIM // LANES):
                g = hh * (LRU_HEAD_DIM // LANES) + gg
                lanes = slice(gg * LANES, (gg + 1) * LANES)
                a_ref[pl.ds(g * CHUNK_PITCH, rows), :] = a[:, lanes]
                b_ref[pl.ds(g * CHUNK_PITCH, rows), :] = b[:, lanes]

    def latent_chunk(k):
        s = pl.multiple_of(k * LRU_CHUNK, LRU_CHUNK)
        s_lo = pl.multiple_of(jnp.maximum(s - HALO, 0), HALO)
        s_hi = pl.multiple_of(jnp.minimum(s + LRU_CHUNK, seq_len - HALO), HALO)
        lo = xa_ref[0, pl.ds(s_lo, HALO), :].astype(F32)
        hi = xa_ref[0, pl.ds(s_hi, HALO), :].astype(F32)
        lo = jnp.where(k > 0, lo, 0.0)
        hi = jnp.where(k < n_chunks - 1, hi, 0.0)
        mid = xa_ref[0, pl.ds(s, LRU_CHUNK), :].astype(F32)
        return conv(jnp.concatenate([lo, mid, hi], axis=0), LRU_CHUNK)

    def load_step(refs, j):
        a_ref, b_ref = refs
        return (a_ref[pl.ds(j, LRU_GROUPS, stride=CHUNK_PITCH), :],
                b_ref[pl.ds(j, LRU_GROUPS, stride=CHUNK_PITCH), :])

    zpad = jnp.zeros((HALO, LRU_TILE), F32)
    u_ctx = conv(jnp.concatenate([zpad, xc_ref[0].astype(F32), zpad], axis=0), ctx_len)
    coefficients(u_ctx, 0, ctx_len)
    coefficients(u_ctx, 1, ctx_len)

    def ctx_body(j, carry):
        hf, hr = carry
        a, b = load_step(coef_refs[0], j)
        hf = a * hf + b
        a, b = load_step(coef_refs[1], ctx_len - 1 - j)
        hr = a * hr + b
        return hf, hr

    h0 = jnp.zeros((LRU_GROUPS, LANES), F32)
    carry = lax.fori_loop(0, ctx_len, ctx_body, (h0, h0), unroll=8)

    def scan_pair(k, carry, second_half):
        kr = n_chunks - 1 - k
        sf = pl.multiple_of(k * LRU_CHUNK, LRU_CHUNK)
        sr = pl.multiple_of(kr * LRU_CHUNK, LRU_CHUNK)
        if second_half:
            coefficients(u_ref[pl.ds(sf, LRU_CHUNK), :], 0, LRU_CHUNK)
            coefficients(u_ref[pl.ds(sr, LRU_CHUNK), :], 1, LRU_CHUNK)
        else:
            uf = latent_chunk(k)
            u_ref[pl.ds(sf, LRU_CHUNK), :] = uf
            coefficients(uf, 0, LRU_CHUNK)
            ur = latent_chunk(kr)
            u_ref[pl.ds(sr, LRU_CHUNK), :] = ur
            coefficients(ur, 1, LRU_CHUNK)

        def body(j, c):
            hf, hr = c
            a, b = load_step(coef_refs[0], j)
            hf = a * hf + b
            jr = LRU_CHUNK - 1 - j
            a, b = load_step(coef_refs[1], jr)
            hr = a * hr + b
            if second_half:
                hf_ref[pl.ds(j, LRU_GROUPS, stride=CHUNK_PITCH), :] = hf
                hr_ref[pl.ds(jr, LRU_GROUPS, stride=CHUNK_PITCH), :] = hr
            else:
                y_ref[pl.ds(sf + j, LRU_GROUPS, stride=y_pitch), :] = hf
                y_ref[pl.ds(sr + jr, LRU_GROUPS, stride=y_pitch), :] = hr
            return hf, hr

        carry = lax.fori_loop(0, LRU_CHUNK, body, carry, unroll=8)

        if second_half:
            for s, h_ref in ((sf, hf_ref), (sr, hr_ref)):
                y = jnp.concatenate(
                    [y_ref[pl.ds(g * y_pitch + s, LRU_CHUNK), :]
                     + h_ref[pl.ds(g * CHUNK_PITCH, LRU_CHUNK), :] for g in range(LRU_GROUPS)],
                    axis=1)
                gate = ga_ref[0, pl.ds(s, LRU_CHUNK), :].astype(F32)
                o_ref[0, pl.ds(s, LRU_CHUNK), :] = (y * _silu(gate)).astype(o_ref.dtype)
        return carry

    half = n_chunks // 2
    carry = lax.fori_loop(0, half, functools.partial(scan_pair, second_half=False), carry)
    lax.fori_loop(half, n_chunks, functools.partial(scan_pair, second_half=True), carry)


def _lru(proj, proj_c, conv_w, conv_b, lam, wg, b_r, b_i):
    bsz, seq_len, _ = proj.shape
    ctx_len = proj_c.shape[1]
    assert seq_len % (2 * LRU_CHUNK) == 0 and ctx_len <= LRU_CHUNK and ctx_len % HALO == 0
    n_tiles = LRU_W // LRU_TILE
    heads = LRU_TILE // LRU_HEAD_DIM
    gate_block0 = (MIX_W) // LRU_TILE
    slab = pltpu.VMEM((LRU_GROUPS * CHUNK_PITCH, LANES), F32)
    kern = functools.partial(_lru_kernel, seq_len=seq_len, ctx_len=ctx_len)
    return pl.pallas_call(
        kern,
        grid=(bsz, n_tiles),
        in_specs=[
            pl.BlockSpec((1, seq_len, LRU_TILE), lambda b, t: (b, 0, t)),
            pl.BlockSpec((1, seq_len, LRU_TILE), lambda b, t: (b, 0, gate_block0 + t)),
            pl.BlockSpec((1, ctx_len, LRU_TILE), lambda b, t: (b, 0, t)),
            pl.BlockSpec((CONV_W, LRU_TILE), lambda b, t: (0, t)),
            pl.BlockSpec((1, LRU_TILE), lambda b, t: (0, t)),
            pl.BlockSpec((2, LRU_TILE), lambda b, t: (0, t)),
            pl.BlockSpec((2, heads, LRU_HEAD_DIM, 2 * LRU_HEAD_DIM), lambda b, t: (0, t, 0, 0)),
            pl.BlockSpec((2, LRU_TILE), lambda b, t: (0, t)),
            pl.BlockSpec((2, LRU_TILE), lambda b, t: (0, t)),
        ],
        out_specs=pl.BlockSpec((1, seq_len, LRU_TILE), lambda b, t: (b, 0, t)),
        out_shape=jax.ShapeDtypeStruct((bsz, seq_len, LRU_W), BF16),
        scratch_shapes=[slab, slab, slab, slab, slab, slab,
                        pltpu.VMEM((LRU_GROUPS * (seq_len + 8), LANES), F32),
                        pltpu.VMEM((seq_len, LRU_TILE), F32)],
        compiler_params=pltpu.CompilerParams(
            dimension_semantics=("parallel", "parallel"), vmem_limit_bytes=VMEM_LIMIT_BYTES),
        name="lru",
    )(proj, proj, proj_c, conv_w, conv_b.reshape(1, LRU_W), lam, wg, b_r, b_i)


POOL_ROWS = 512
POOL_BAND = 256


def _pool_operators():
    t = np.arange(POOL_BAND)
    seg, pos = t // GRID_W, t % GRID_W
    wins, inv = [], []
    for w in POOL_WINDOWS:
        left = w // 2
        right = w - 1 - left
        lo = np.maximum(pos - left, 0)
        hi = np.minimum(pos + right, GRID_W - 1)
        member = ((seg[:, None] == seg[None, :])
                  & (pos[None, :] >= lo[:, None]) & (pos[None, :] <= hi[:, None]))
        wins.append(member.astype(np.float32))
        inv.append(np.broadcast_to((1.0 / (hi - lo + 1).astype(np.float32))[:, None],
                                   (POOL_BAND, LANES)))
    return jnp.asarray(np.stack(wins), BF16), jnp.asarray(np.stack(inv), F32)


def _pool_kernel(xb_ref, gb_ref, win_ref, inv_ref, wp_ref, bp_ref, sc_ref, o_ref):
    inv = jnp.tile(inv_ref[0], (1, POOL_GROUP_DIM // LANES))
    zs = []
    for sub in range(POOL_ROWS // POOL_BAND):
        xs = xb_ref[0, sub * POOL_BAND:(sub + 1) * POOL_BAND, :]
        sums = jnp.dot(win_ref[0], xs, preferred_element_type=F32)
        zs.append((sums * inv - xs.astype(F32)).astype(BF16))
    z = jnp.concatenate(zs, axis=0)
    y = (jnp.dot(z, wp_ref[0], preferred_element_type=F32) + bp_ref[...]) * sc_ref[...]
    o_ref[0] = (y * _silu(gb_ref[0].astype(F32))).astype(o_ref.dtype)


def _pool(proj, wp_bf16, b_pool, pool_scale):
    bsz, seq_len, _ = proj.shape
    assert seq_len % POOL_ROWS == 0 and POOL_BAND % GRID_W == 0
    groups = len(POOL_WINDOWS)
    xb_block0 = LRU_W // POOL_GROUP_DIM
    gb_block0 = (MIX_W + LRU_W) // POOL_GROUP_DIM
    win, inv = _pool_operators()
    return pl.pallas_call(
        _pool_kernel,
        grid=(groups, bsz, seq_len // POOL_ROWS),
        in_specs=[
            pl.BlockSpec((1, POOL_ROWS, POOL_GROUP_DIM), lambda g, b, l: (b, l, xb_block0 + g)),
            pl.BlockSpec((1, POOL_ROWS, POOL_GROUP_DIM), lambda g, b, l: (b, l, gb_block0 + g)),
            pl.BlockSpec((1, POOL_BAND, POOL_BAND), lambda g, b, l: (g, 0, 0)),
            pl.BlockSpec((1, POOL_BAND, LANES), lambda g, b, l: (g, 0, 0)),
            pl.BlockSpec((1, POOL_GROUP_DIM, POOL_GROUP_DIM), lambda g, b, l: (g, 0, 0)),
            pl.BlockSpec((1, POOL_GROUP_DIM), lambda g, b, l: (0, g)),
            pl.BlockSpec((1, POOL_GROUP_DIM), lambda g, b, l: (0, g)),
        ],
        out_specs=pl.BlockSpec((1, POOL_ROWS, POOL_GROUP_DIM), lambda g, b, l: (b, l, g)),
        out_shape=jax.ShapeDtypeStruct((bsz, seq_len, POOL_W), BF16),
        compiler_params=pltpu.CompilerParams(
            dimension_semantics=("parallel", "parallel", "parallel"),
            vmem_limit_bytes=VMEM_LIMIT_BYTES),
        name="pool",
    )(proj, proj, win, inv, wp_bf16, b_pool.reshape(1, POOL_W), pool_scale.reshape(1, POOL_W))


def _outproj_kernel(ma_ref, mb_ref, w_ref, x_ref, gate_ref, gf_ref, o_ref, *, k_half):
    k = pl.program_id(1)

    @pl.when(k == 0)
    def _():
        o_ref[...] = jnp.dot(ma_ref[...], w_ref[...], preferred_element_type=F32)

    @pl.when((k > 0) & (k < k_half))
    def _():
        o_ref[...] += jnp.dot(ma_ref[...], w_ref[...], preferred_element_type=F32)

    @pl.when(k >= k_half)
    def _():
        o_ref[...] += jnp.dot(mb_ref[...], w_ref[...], preferred_element_type=F32)

    @pl.when(k == 2 * k_half - 1)
    def _():
        xn = x_ref[...] + gate_ref[0] * o_ref[...]
        ms = jnp.mean(xn * xn, axis=-1, keepdims=True)
        o_ref[...] = xn * lax.rsqrt(ms + EPS) * gf_ref[...]


def _out_proj(ma, mb, w_bf16, x2d, gate, g_final, rows_per_mod, tm=512, tk=512):
    m, d = x2d.shape
    k_half = ma.shape[1] // tk
    tiles_per_mod = rows_per_mod // tm
    kern = functools.partial(_outproj_kernel, k_half=k_half)
    return pl.pallas_call(
        kern,
        grid=(m // tm, 2 * k_half),
        in_specs=[
            pl.BlockSpec((tm, tk), lambda i, k: (i, jnp.minimum(k, k_half - 1))),
            pl.BlockSpec((tm, tk), lambda i, k: (i, jnp.maximum(k - k_half, 0))),
            pl.BlockSpec((tk, d), lambda i, k: (k, 0)),
            pl.BlockSpec((tm, d), lambda i, k: (i, 0)),
            pl.BlockSpec((1, 1, d), lambda i, k: (i // tiles_per_mod, 0, 0)),
            pl.BlockSpec((1, d), lambda i, k: (0, 0)),
        ],
        out_specs=pl.BlockSpec((tm, d), lambda i, k: (i, 0)),
        out_shape=jax.ShapeDtypeStruct((m, d), F32),
        compiler_params=pltpu.CompilerParams(
            dimension_semantics=("parallel", "arbitrary"), vmem_limit_bytes=VMEM_LIMIT_BYTES),
        name="out_proj",
    )(ma, mb, w_bf16, x2d, gate, g_final.reshape(1, d))


def kernel(x, c, ctx, c_ctx, w_ada, b_ada, g_norm, w_in, conv_w, conv_b, lru_lambda,
           w_rgate, b_rgate, w_igate, b_igate, w_pool, b_pool, pool_scale, w_out, g_final):
    bsz, seq_len, d = x.shape
    ctx_len = ctx.shape[1]
    depth = w_ada.shape[0]
    assert depth == 1 and d == D_MODEL
    l = 0

    cc = jnp.zeros((8, d), F32).at[:bsz].set(c).at[bsz].set(c_ctx)
    mod = _adaln(cc, w_ada[l], b_ada[l])
    shift = mod[:, None, :d]
    scale = mod[:, None, d:2 * d]
    gate = mod[:, None, 2 * d:]

    w_in_b = w_in[l].astype(BF16)
    x2d = x.reshape(bsz * seq_len, d)
    proj = _in_proj(x2d, g_norm[l], shift, scale, w_in_b, seq_len, 2 * MIX_W,
                    ).reshape(bsz, seq_len, 2 * MIX_W)
    proj_c = _in_proj(ctx.reshape(bsz * ctx_len, d), g_norm[l], shift[bsz:bsz + 1],
                      scale[bsz:bsz + 1], w_in_b, bsz * ctx_len, LRU_W,
                      ).reshape(bsz, ctx_len, LRU_W)

    wg = (0.5 * jnp.concatenate([w_rgate[l], w_igate[l]], axis=-1)).astype(BF16)
    mixed_a = _lru(proj, proj_c, conv_w[l], conv_b[l], lru_lambda[l], wg, b_rgate[l], b_igate[l])
    mixed_b = _pool(proj, w_pool[l].astype(BF16), b_pool[l], pool_scale[l])

    out = _out_proj(mixed_a.reshape(bsz * seq_len, LRU_W), mixed_b.reshape(bsz * seq_len, POOL_W),
                    w_out[l].astype(BF16), x2d, gate, g_final, seq_len)
    return out.reshape(bsz, seq_len, d)
```

```python
import functools

import jax
import jax.numpy as jnp
import numpy as np
from jax import lax
from jax.experimental import pallas as pl
from jax.experimental.pallas import tpu as pltpu

D_MODEL = 4096
GRID_W = 64
MIX_W = 2 * D_MODEL
LRU_W = MIX_W // 2
POOL_W = MIX_W - LRU_W
LRU_HEADS = 16
LRU_HEAD_DIM = LRU_W // LRU_HEADS
CONV_W = 4
CONV_LEFT = 1
LRU_C = 8.0
POOL_WINDOWS = (2, 4, 8, 16)
POOL_GROUP_DIM = POOL_W // len(POOL_WINDOWS)
EPS = 1e-6

LANES = 128
VMEM_LIMIT_BYTES = 56 * 1024 * 1024

BF16 = jnp.bfloat16
F32 = jnp.float32


def _sigmoid(v):
    return 0.5 * jnp.tanh(0.5 * v) + 0.5


def _silu(v):
    return v * _sigmoid(v)


def _adaln_kernel(c_ref, w_ref, b_ref, o_ref):
    s = _silu(c_ref[...]).astype(BF16)
    o_ref[...] = jnp.dot(s, w_ref[...].astype(BF16), preferred_element_type=F32) + b_ref[...]


def _adaln(cc, w_ada, b_ada, tn=512):
    rows, d = cc.shape
    n = w_ada.shape[1]
    return pl.pallas_call(
        _adaln_kernel,
        grid=(n // tn,),
        in_specs=[
            pl.BlockSpec((rows, d), lambda j: (0, 0)),
            pl.BlockSpec((d, tn), lambda j: (0, j)),
            pl.BlockSpec((1, tn), lambda j: (0, j)),
        ],
        out_specs=pl.BlockSpec((rows, tn), lambda j: (0, j)),
        out_shape=jax.ShapeDtypeStruct((rows, n), F32),
        compiler_params=pltpu.CompilerParams(
            dimension_semantics=("parallel",), vmem_limit_bytes=VMEM_LIMIT_BYTES),
        name="adaln",
    )(cc, w_ada, b_ada.reshape(1, n))


def _inproj_kernel(x_ref, g_ref, shift_ref, scale_ref, w_ref, o_ref, h_ref, *, n_sub):
    j = pl.program_id(1)
    sub_rows = x_ref.shape[0]

    @pl.when(j < n_sub)
    def _():
        xf = x_ref[...]
        ms = jnp.mean(xf * xf, axis=-1, keepdims=True)
        y = xf * lax.rsqrt(ms + EPS) * g_ref[...]
        h = (y * (1.0 + scale_ref[0]) + shift_ref[0]).astype(BF16)
        h_ref[pl.ds(pl.multiple_of(j * sub_rows, sub_rows), sub_rows), :] = h

    @pl.when(j >= n_sub)
    def _():
        o_ref[...] = jnp.dot(h_ref[...], w_ref[...], preferred_element_type=F32).astype(o_ref.dtype)


def _in_proj(x2d, g_norm, shift, scale, w_bf16, rows_per_mod, n_cols, tm=1024, tn=1024,
             sub_rows=512):
    m, d = x2d.shape
    n_sub = tm // sub_rows
    tiles_per_mod = rows_per_mod // tm
    kern = functools.partial(_inproj_kernel, n_sub=n_sub)
    return pl.pallas_call(
        kern,
        grid=(m // tm, n_sub + n_cols // tn),
        in_specs=[
            pl.BlockSpec((sub_rows, d), lambda i, j: (i * n_sub + jnp.minimum(j, n_sub - 1), 0)),
            pl.BlockSpec((1, d), lambda i, j: (0, 0)),
            pl.BlockSpec((1, 1, d), lambda i, j: (i // tiles_per_mod, 0, 0)),
            pl.BlockSpec((1, 1, d), lambda i, j: (i // tiles_per_mod, 0, 0)),
            pl.BlockSpec((d, tn), lambda i, j: (0, jnp.maximum(j - n_sub, 0))),
        ],
        out_specs=pl.BlockSpec((tm, tn), lambda i, j: (i, jnp.maximum(j - n_sub, 0))),
        out_shape=jax.ShapeDtypeStruct((m, n_cols), BF16),
        scratch_shapes=[pltpu.VMEM((tm, d), BF16)],
        compiler_params=pltpu.CompilerParams(
            dimension_semantics=("parallel", "arbitrary"), vmem_limit_bytes=VMEM_LIMIT_BYTES),
        name="in_proj",
    )(x2d, g_norm.reshape(1, d), shift, scale, w_bf16)


LRU_TILE = 512
LRU_GROUPS = LRU_TILE // LANES
LRU_CHUNK = 256
HALO = 16
CHUNK_PITCH = LRU_CHUNK + 8
SLAB_SET = LRU_GROUPS * CHUNK_PITCH


def _lru_kernel(xa_ref, ga_ref, xc_ref, cw_ref, cb_ref, lam_ref, wg_ref, br_ref, bi_ref,
                o_ref, af0_ref, bf0_ref, ar0_ref, br0_ref, af1_ref, bf1_ref, ar1_ref, br1_ref,
                hf_ref, hr_ref, y_ref, u_ref, *, seq_len, ctx_len):
    n_chunks = seq_len // LRU_CHUNK
    y_pitch = seq_len + 8
    heads = LRU_TILE // LRU_HEAD_DIM
    coef_sets = (((af0_ref, bf0_ref), (ar0_ref, br0_ref)),
                 ((af1_ref, bf1_ref), (ar1_ref, br1_ref)))

    cw = cw_ref[...]
    cb = cb_ref[...]
    half_decay = (-0.5 * LRU_C) * jax.nn.softplus(-lam_ref[...])
    half_br = 0.5 * br_ref[...]
    half_bi = 0.5 * bi_ref[...]

    def conv(x_pad, rows):
        n = rows + 2 * HALO
        u = cb + x_pad[HALO:HALO + rows] * cw[CONV_LEFT:CONV_LEFT + 1]
        for k in range(CONV_W):
            off = k - CONV_LEFT
            if off == 0:
                continue
            u = u + pltpu.roll(x_pad, (-off) % n, axis=0)[HALO:HALO + rows] * cw[k:k + 1]
        return u

    def coefficients(u, d, rows, cset):
        a_ref, b_ref = coef_sets[cset][d]
        for hh in range(heads):
            cols = slice(hh * LRU_HEAD_DIM, (hh + 1) * LRU_HEAD_DIM)
            uh = u[:, cols]
            gates = jnp.dot(uh.astype(BF16), wg_ref[d, hh], preferred_element_type=F32)
            tr = jnp.tanh(gates[:, :LRU_HEAD_DIM] + half_br[d:d + 1, cols])
            ti = jnp.tanh(gates[:, LRU_HEAD_DIM:] + half_bi[d:d + 1, cols])
            i = 0.5 * ti + 0.5
            log_a = half_decay[d:d + 1, cols] * tr + half_decay[d:d + 1, cols]
            a = jnp.exp(log_a)
            t = jnp.tanh(log_a)
            p = -2.0 * t
            q = 1.0 - t
            root = jnp.where(p > 0.0, p * lax.rsqrt(p * q), 0.0)
            b = root * (i * uh)
            for gg in range(LRU_HEAD_DIM // LANES):
                g = hh * (LRU_HEAD_DIM // LANES) + gg
                lanes = slice(gg * LANES, (gg + 1) * LANES)
                a_ref[pl.ds(g * CHUNK_PITCH, rows), :] = a[:, lanes]
                b_ref[pl.ds(g * CHUNK_PITCH, rows), :] = b[:, lanes]

    def latent_chunk(k):
        s = pl.multiple_of(k * LRU_CHUNK, LRU_CHUNK)
        s_lo = pl.multiple_of(jnp.maximum(s - HALO, 0), HALO)
        s_hi = pl.multiple_of(jnp.minimum(s + LRU_CHUNK, seq_len - HALO), HALO)
        lo = xa_ref[0, pl.ds(s_lo, HALO), :].astype(F32)
        hi = xa_ref[0, pl.ds(s_hi, HALO), :].astype(F32)
        lo = jnp.where(k > 0, lo, 0.0)
        hi = jnp.where(k < n_chunks - 1, hi, 0.0)
        mid = xa_ref[0, pl.ds(s, LRU_CHUNK), :].astype(F32)
        return conv(jnp.concatenate([lo, mid, hi], axis=0), LRU_CHUNK)

    def load_step(refs, row):
        a_ref, b_ref = refs
        return (a_ref[pl.ds(row, LRU_GROUPS, stride=CHUNK_PITCH), :],
                b_ref[pl.ds(row, LRU_GROUPS, stride=CHUNK_PITCH), :])

    def scan_steps(cset, carry, steps, store):
        hf, hr = carry
        for j in range(steps):
            jr = steps - 1 - j
            a, b = load_step(coef_sets[cset][0], j)
            hf = a * hf + b
            a, b = load_step(coef_sets[cset][1], jr)
            hr = a * hr + b
            if store is not None:
                store(j, jr, hf, hr)
        return hf, hr

    def prepare_from_conv(k, cset):
        kr = n_chunks - 1 - k
        uf = latent_chunk(k)
        u_ref[pl.ds(pl.multiple_of(k * LRU_CHUNK, LRU_CHUNK), LRU_CHUNK), :] = uf
        coefficients(uf, 0, LRU_CHUNK, cset)
        ur = latent_chunk(kr)
        u_ref[pl.ds(pl.multiple_of(kr * LRU_CHUNK, LRU_CHUNK), LRU_CHUNK), :] = ur
        coefficients(ur, 1, LRU_CHUNK, cset)

    def prepare_from_cache(k, cset):
        kr = n_chunks - 1 - k
        coefficients(u_ref[pl.ds(pl.multiple_of(k * LRU_CHUNK, LRU_CHUNK), LRU_CHUNK), :],
                     0, LRU_CHUNK, cset)
        coefficients(u_ref[pl.ds(pl.multiple_of(kr * LRU_CHUNK, LRU_CHUNK), LRU_CHUNK), :],
                     1, LRU_CHUNK, cset)

    zpad = jnp.zeros((HALO, LRU_TILE), F32)
    u_ctx = conv(jnp.concatenate([zpad, xc_ref[0].astype(F32), zpad], axis=0), ctx_len)
    coefficients(u_ctx, 0, ctx_len, 1)
    coefficients(u_ctx, 1, ctx_len, 1)
    prepare_from_conv(jnp.int32(0), 0)
    h0 = jnp.zeros((LRU_GROUPS, LANES), F32)
    carry = scan_steps(1, (h0, h0), ctx_len, None)

    def scan_first_half(k, cset, carry):
        sf = k * LRU_CHUNK
        sr = (n_chunks - 1 - k) * LRU_CHUNK

        def store(j, jr, hf, hr):
            y_ref[pl.ds(sf + j, LRU_GROUPS, stride=y_pitch), :] = hf
            y_ref[pl.ds(sr + jr, LRU_GROUPS, stride=y_pitch), :] = hr

        return scan_steps(cset, carry, LRU_CHUNK, store)

    def scan_second_half(k, cset, carry):
        def store(j, jr, hf, hr):
            hf_ref[pl.ds(j, LRU_GROUPS, stride=CHUNK_PITCH), :] = hf
            hr_ref[pl.ds(jr, LRU_GROUPS, stride=CHUNK_PITCH), :] = hr

        carry = scan_steps(cset, carry, LRU_CHUNK, store)
        for kk, h_ref in ((k, hf_ref), (n_chunks - 1 - k, hr_ref)):
            s = pl.multiple_of(kk * LRU_CHUNK, LRU_CHUNK)
            y = jnp.concatenate(
                [y_ref[pl.ds(g * y_pitch + s, LRU_CHUNK), :]
                 + h_ref[pl.ds(g * CHUNK_PITCH, LRU_CHUNK), :] for g in range(LRU_GROUPS)],
                axis=1)
            gate = ga_ref[0, pl.ds(s, LRU_CHUNK), :].astype(F32)
            o_ref[0, pl.ds(s, LRU_CHUNK), :] = (y * _silu(gate)).astype(o_ref.dtype)
        return carry

    def first_half(m, carry):
        k = 2 * m
        prepare_from_conv(k + 1, 1)
        carry = scan_first_half(k, 0, carry)
        prepare_from_conv(k + 2, 0)
        return scan_first_half(k + 1, 1, carry)

    def second_half(m, carry):
        k = 2 * m
        prepare_from_cache(k + 1, 1)
        carry = scan_second_half(k, 0, carry)
        prepare_from_cache(jnp.minimum(k + 2, n_chunks - 1), 0)
        return scan_second_half(k + 1, 1, carry)

    quarter = n_chunks // 4
    carry = lax.fori_loop(0, quarter, first_half, carry)
    lax.fori_loop(quarter, 2 * quarter, second_half, carry)


def _lru(proj, proj_c, conv_w, conv_b, lam, wg, b_r, b_i):
    bsz, seq_len, _ = proj.shape
    ctx_len = proj_c.shape[1]
    assert seq_len % (4 * LRU_CHUNK) == 0 and ctx_len <= LRU_CHUNK and ctx_len % HALO == 0
    n_tiles = LRU_W // LRU_TILE
    heads = LRU_TILE // LRU_HEAD_DIM
    gate_block0 = (MIX_W) // LRU_TILE
    slab = pltpu.VMEM((SLAB_SET, LANES), F32)
    kern = functools.partial(_lru_kernel, seq_len=seq_len, ctx_len=ctx_len)
    return pl.pallas_call(
        kern,
        grid=(bsz, n_tiles),
        in_specs=[
            pl.BlockSpec((1, seq_len, LRU_TILE), lambda b, t: (b, 0, t)),
            pl.BlockSpec((1, seq_len, LRU_TILE), lambda b, t: (b, 0, gate_block0 + t)),
            pl.BlockSpec((1, ctx_len, LRU_TILE), lambda b, t: (b, 0, t)),
            pl.BlockSpec((CONV_W, LRU_TILE), lambda b, t: (0, t)),
            pl.BlockSpec((1, LRU_TILE), lambda b, t: (0, t)),
            pl.BlockSpec((2, LRU_TILE), lambda b, t: (0, t)),
            pl.BlockSpec((2, heads, LRU_HEAD_DIM, 2 * LRU_HEAD_DIM), lambda b, t: (0, t, 0, 0)),
            pl.BlockSpec((2, LRU_TILE), lambda b, t: (0, t)),
            pl.BlockSpec((2, LRU_TILE), lambda b, t: (0, t)),
        ],
        out_specs=pl.BlockSpec((1, seq_len, LRU_TILE), lambda b, t: (b, 0, t)),
        out_shape=jax.ShapeDtypeStruct((bsz, seq_len, LRU_W), BF16),
        scratch_shapes=[slab] * 10 + [
                        pltpu.VMEM((LRU_GROUPS * (seq_len + 8), LANES), F32),
                        pltpu.VMEM((seq_len, LRU_TILE), F32)],
        compiler_params=pltpu.CompilerParams(
            dimension_semantics=("parallel", "parallel"), vmem_limit_bytes=VMEM_LIMIT_BYTES),
        name="lru",
    )(proj, proj, proj_c, conv_w, conv_b.reshape(1, LRU_W), lam, wg, b_r, b_i)


POOL_ROWS = 1024
POOL_BAND = 256


def _pool_operators():
    t = np.arange(POOL_BAND)
    seg, pos = t // GRID_W, t % GRID_W
    wins, inv = [], []
    for w in POOL_WINDOWS:
        left = w // 2
        right = w - 1 - left
        lo = np.maximum(pos - left, 0)
        hi = np.minimum(pos + right, GRID_W - 1)
        member = ((seg[:, None] == seg[None, :])
                  & (pos[None, :] >= lo[:, None]) & (pos[None, :] <= hi[:, None]))
        wins.append(member.astype(np.float32))
        inv.append(np.broadcast_to((1.0 / (hi - lo + 1).astype(np.float32))[:, None],
                                   (POOL_BAND, LANES)))
    return jnp.asarray(np.stack(wins), BF16), jnp.asarray(np.stack(inv), F32)


def _pool_kernel(xb_ref, gb_ref, win_ref, inv_ref, wp_ref, bp_ref, sc_ref, o_ref):
    inv = jnp.tile(inv_ref[0], (1, POOL_GROUP_DIM // LANES))
    zs = []
    for sub in range(POOL_ROWS // POOL_BAND):
        xs = xb_ref[0, sub * POOL_BAND:(sub + 1) * POOL_BAND, :]
        sums = jnp.dot(win_ref[0], xs, preferred_element_type=F32)
        zs.append((sums * inv - xs.astype(F32)).astype(BF16))
    z = jnp.concatenate(zs, axis=0)
    y = (jnp.dot(z, wp_ref[0], preferred_element_type=F32) + bp_ref[...]) * sc_ref[...]
    o_ref[0] = (y * _silu(gb_ref[0].astype(F32))).astype(o_ref.dtype)


def _pool(proj, wp_bf16, b_pool, pool_scale):
    bsz, seq_len, _ = proj.shape
    assert seq_len % POOL_ROWS == 0 and POOL_BAND % GRID_W == 0
    groups = len(POOL_WINDOWS)
    xb_block0 = LRU_W // POOL_GROUP_DIM
    gb_block0 = (MIX_W + LRU_W) // POOL_GROUP_DIM
    win, inv = _pool_operators()
    return pl.pallas_call(
        _pool_kernel,
        grid=(groups, bsz, seq_len // POOL_ROWS),
        in_specs=[
            pl.BlockSpec((1, POOL_ROWS, POOL_GROUP_DIM), lambda g, b, l: (b, l, xb_block0 + g)),
            pl.BlockSpec((1, POOL_ROWS, POOL_GROUP_DIM), lambda g, b, l: (b, l, gb_block0 + g)),
            pl.BlockSpec((1, POOL_BAND, POOL_BAND), lambda g, b, l: (g, 0, 0)),
            pl.BlockSpec((1, POOL_BAND, LANES), lambda g, b, l: (g, 0, 0)),
            pl.BlockSpec((1, POOL_GROUP_DIM, POOL_GROUP_DIM), lambda g, b, l: (g, 0, 0)),
            pl.BlockSpec((1, POOL_GROUP_DIM), lambda g, b, l: (0, g)),
            pl.BlockSpec((1, POOL_GROUP_DIM), lambda g, b, l: (0, g)),
        ],
        out_specs=pl.BlockSpec((1, POOL_ROWS, POOL_GROUP_DIM), lambda g, b, l: (b, l, g)),
        out_shape=jax.ShapeDtypeStruct((bsz, seq_len, POOL_W), BF16),
        compiler_params=pltpu.CompilerParams(
            dimension_semantics=("parallel", "parallel", "parallel"),
            vmem_limit_bytes=VMEM_LIMIT_BYTES),
        name="pool",
    )(proj, proj, win, inv, wp_bf16, b_pool.reshape(1, POOL_W), pool_scale.reshape(1, POOL_W))


def _outproj_kernel(ma_ref, mb_ref, w_ref, x_ref, gate_ref, gf_ref, o_ref, *, k_half):
    k = pl.program_id(1)

    @pl.when(k == 0)
    def _():
        o_ref[...] = jnp.dot(ma_ref[...], w_ref[...], preferred_element_type=F32)

    @pl.when((k > 0) & (k < k_half))
    def _():
        o_ref[...] += jnp.dot(ma_ref[...], w_ref[...], preferred_element_type=F32)

    @pl.when(k >= k_half)
    def _():
        o_ref[...] += jnp.dot(mb_ref[...], w_ref[...], preferred_element_type=F32)

    @pl.when(k == 2 * k_half - 1)
    def _():
        xn = x_ref[...] + gate_ref[0] * o_ref[...]
        ms = jnp.mean(xn * xn, axis=-1, keepdims=True)
        o_ref[...] = xn * lax.rsqrt(ms + EPS) * gf_ref[...]


def _out_proj(ma, mb, w_bf16, x2d, gate, g_final, rows_per_mod, tm=512, tk=512):
    m, d = x2d.shape
    k_half = ma.shape[1] // tk
    tiles_per_mod = rows_per_mod // tm
    kern = functools.partial(_outproj_kernel, k_half=k_half)
    return pl.pallas_call(
        kern,
        grid=(m // tm, 2 * k_half),
        in_specs=[
            pl.BlockSpec((tm, tk), lambda i, k: (i, jnp.minimum(k, k_half - 1))),
            pl.BlockSpec((tm, tk), lambda i, k: (i, jnp.maximum(k - k_half, 0))),
            pl.BlockSpec((tk, d), lambda i, k: (k, 0)),
            pl.BlockSpec((tm, d), lambda i, k: (i, 0)),
            pl.BlockSpec((1, 1, d), lambda i, k: (i // tiles_per_mod, 0, 0)),
            pl.BlockSpec((1, d), lambda i, k: (0, 0)),
        ],
        out_specs=pl.BlockSpec((tm, d), lambda i, k: (i, 0)),
        out_shape=jax.ShapeDtypeStruct((m, d), F32),
        compiler_params=pltpu.CompilerParams(
            dimension_semantics=("parallel", "arbitrary"), vmem_limit_bytes=VMEM_LIMIT_BYTES),
        name="out_proj",
    )(ma, mb, w_bf16, x2d, gate, g_final.reshape(1, d))


def kernel(x, c, ctx, c_ctx, w_ada, b_ada, g_norm, w_in, conv_w, conv_b, lru_lambda,
           w_rgate, b_rgate, w_igate, b_igate, w_pool, b_pool, pool_scale, w_out, g_final):
    bsz, seq_len, d = x.shape
    ctx_len = ctx.shape[1]
    depth = w_ada.shape[0]
    assert depth == 1 and d == D_MODEL
    l = 0

    cc = jnp.zeros((8, d), F32).at[:bsz].set(c).at[bsz].set(c_ctx)
    mod = _adaln(cc, w_ada[l], b_ada[l])
    shift = mod[:, None, :d]
    scale = mod[:, None, d:2 * d]
    gate = mod[:, None, 2 * d:]

    w_in_b = w_in[l].astype(BF16)
    x2d = x.reshape(bsz * seq_len, d)
    proj = _in_proj(x2d, g_norm[l], shift, scale, w_in_b, seq_len, 2 * MIX_W,
                    ).reshape(bsz, seq_len, 2 * MIX_W)
    proj_c = _in_proj(ctx.reshape(bsz * ctx_len, d), g_norm[l], shift[bsz:bsz + 1],
                      scale[bsz:bsz + 1], w_in_b, bsz * ctx_len, LRU_W,
                      ).reshape(bsz, ctx_len, LRU_W)

    wg = (0.5 * jnp.concatenate([w_rgate[l], w_igate[l]], axis=-1)).astype(BF16)
    mixed_a = _lru(proj, proj_c, conv_w[l], conv_b[l], lru_lambda[l], wg, b_rgate[l], b_igate[l])
    mixed_b = _pool(proj, w_pool[l].astype(BF16), b_pool[l], pool_scale[l])

    out = _out_proj(mixed_a.reshape(bsz * seq_len, LRU_W), mixed_b.reshape(bsz * seq_len, POOL_W),
                    w_out[l].astype(BF16), x2d, gate, g_final, seq_len)
    return out.reshape(bsz, seq_len, d)
```

```python
import functools

import jax
import jax.numpy as jnp
import numpy as np
from jax import lax
from jax.experimental import pallas as pl
from jax.experimental.pallas import tpu as pltpu

D_MODEL = 4096
GRID_W = 64
MIX_W = 2 * D_MODEL
LRU_W = MIX_W // 2
POOL_W = MIX_W - LRU_W
LRU_HEADS = 16
LRU_HEAD_DIM = LRU_W // LRU_HEADS
CONV_W = 4
CONV_LEFT = 1
LRU_C = 8.0
POOL_WINDOWS = (2, 4, 8, 16)
POOL_GROUP_DIM = POOL_W // len(POOL_WINDOWS)
EPS = 1e-6

LANES = 128
VMEM_LIMIT_BYTES = 56 * 1024 * 1024

BF16 = jnp.bfloat16
F32 = jnp.float32


def _sigmoid(v):
    return 0.5 * jnp.tanh(0.5 * v) + 0.5


def _silu(v):
    return v * _sigmoid(v)


def _adaln_kernel(c_ref, w_ref, b_ref, o_ref):
    s = _silu(c_ref[...]).astype(BF16)
    o_ref[...] = jnp.dot(s, w_ref[...].astype(BF16), preferred_element_type=F32) + b_ref[...]


def _adaln(cc, w_ada, b_ada, tn=512):
    rows, d = cc.shape
    n = w_ada.shape[1]
    return pl.pallas_call(
        _adaln_kernel,
        grid=(n // tn,),
        in_specs=[
            pl.BlockSpec((rows, d), lambda j: (0, 0)),
            pl.BlockSpec((d, tn), lambda j: (0, j)),
            pl.BlockSpec((1, tn), lambda j: (0, j)),
        ],
        out_specs=pl.BlockSpec((rows, tn), lambda j: (0, j)),
        out_shape=jax.ShapeDtypeStruct((rows, n), F32),
        compiler_params=pltpu.CompilerParams(
            dimension_semantics=("parallel",), vmem_limit_bytes=VMEM_LIMIT_BYTES),
        name="adaln",
    )(cc, w_ada, b_ada.reshape(1, n))


def _inproj_kernel(x_ref, g_ref, shift_ref, scale_ref, w_ref, o_ref, h_ref, *, n_sub):
    j = pl.program_id(1)
    sub_rows = x_ref.shape[0]

    @pl.when(j < n_sub)
    def _():
        xf = x_ref[...]
        ms = jnp.mean(xf * xf, axis=-1, keepdims=True)
        y = xf * lax.rsqrt(ms + EPS) * g_ref[...]
        h = (y * (1.0 + scale_ref[0]) + shift_ref[0]).astype(BF16)
        h_ref[pl.ds(pl.multiple_of(j * sub_rows, sub_rows), sub_rows), :] = h

    @pl.when(j >= n_sub)
    def _():
        o_ref[...] = jnp.dot(h_ref[...], w_ref[...], preferred_element_type=F32).astype(o_ref.dtype)


def _in_proj(x2d, g_norm, shift, scale, w_bf16, rows_per_mod, n_cols, tm=1024, tn=1024,
             sub_rows=512):
    m, d = x2d.shape
    n_sub = tm // sub_rows
    tiles_per_mod = rows_per_mod // tm
    kern = functools.partial(_inproj_kernel, n_sub=n_sub)
    return pl.pallas_call(
        kern,
        grid=(m // tm, n_sub + n_cols // tn),
        in_specs=[
            pl.BlockSpec((sub_rows, d), lambda i, j: (i * n_sub + jnp.minimum(j, n_sub - 1), 0)),
            pl.BlockSpec((1, d), lambda i, j: (0, 0)),
            pl.BlockSpec((1, 1, d), lambda i, j: (i // tiles_per_mod, 0, 0)),
            pl.BlockSpec((1, 1, d), lambda i, j: (i // tiles_per_mod, 0, 0)),
            pl.BlockSpec((d, tn), lambda i, j: (0, jnp.maximum(j - n_sub, 0))),
        ],
        out_specs=pl.BlockSpec((tm, tn), lambda i, j: (i, jnp.maximum(j - n_sub, 0))),
        out_shape=jax.ShapeDtypeStruct((m, n_cols), BF16),
        scratch_shapes=[pltpu.VMEM((tm, d), BF16)],
        compiler_params=pltpu.CompilerParams(
            dimension_semantics=("parallel", "arbitrary"), vmem_limit_bytes=VMEM_LIMIT_BYTES),
        name="in_proj",
    )(x2d, g_norm.reshape(1, d), shift, scale, w_bf16)


LRU_TILE = 512
LRU_GROUPS = LRU_TILE // LANES
LRU_CHUNK = 256
HALO = 16
CHUNK_PITCH = LRU_CHUNK + 8
SLAB_SET = LRU_GROUPS * CHUNK_PITCH


def _lru_kernel(xa_ref, ga_ref, xc_ref, cw_ref, cb_ref, lam_ref, wg_ref, br_ref, bi_ref,
                o_ref, af0_ref, bf0_ref, ar0_ref, br0_ref, af1_ref, bf1_ref, ar1_ref, br1_ref,
                hf_ref, hr_ref, y_ref, u_ref, *, seq_len, ctx_len):
    n_chunks = seq_len // LRU_CHUNK
    y_pitch = seq_len + 8
    heads = LRU_TILE // LRU_HEAD_DIM
    coef_sets = (((af0_ref, bf0_ref), (ar0_ref, br0_ref)),
                 ((af1_ref, bf1_ref), (ar1_ref, br1_ref)))

    cw = cw_ref[...]
    cb = cb_ref[...]
    half_decay = (-0.5 * LRU_C) * jax.nn.softplus(-lam_ref[...])
    half_br = 0.5 * br_ref[...]
    half_bi = 0.5 * bi_ref[...]

    def conv(x_pad, rows):
        n = rows + 2 * HALO
        u = cb + x_pad[HALO:HALO + rows] * cw[CONV_LEFT:CONV_LEFT + 1]
        for k in range(CONV_W):
            off = k - CONV_LEFT
            if off == 0:
                continue
            u = u + pltpu.roll(x_pad, (-off) % n, axis=0)[HALO:HALO + rows] * cw[k:k + 1]
        return u

    def coefficients(u, d, rows, cset):
        a_ref, b_ref = coef_sets[cset][d]
        for hh in range(heads):
            cols = slice(hh * LRU_HEAD_DIM, (hh + 1) * LRU_HEAD_DIM)
            uh = u[:, cols]
            gates = jnp.dot(uh.astype(BF16), wg_ref[d, hh], preferred_element_type=F32)
            tr = jnp.tanh(gates[:, :LRU_HEAD_DIM] + half_br[d:d + 1, cols])
            ti = jnp.tanh(gates[:, LRU_HEAD_DIM:] + half_bi[d:d + 1, cols])
            i = 0.5 * ti + 0.5
            log_a = half_decay[d:d + 1, cols] * tr + half_decay[d:d + 1, cols]
            a = jnp.exp(log_a)
            t = jnp.tanh(log_a)
            p = -2.0 * t
            q = 1.0 - t
            root = jnp.where(p > 0.0, p * lax.rsqrt(p * q), 0.0)
            b = root * (i * uh)
            for gg in range(LRU_HEAD_DIM // LANES):
                g = hh * (LRU_HEAD_DIM // LANES) + gg
                lanes = slice(gg * LANES, (gg + 1) * LANES)
                a_ref[pl.ds(g * CHUNK_PITCH, rows), :] = a[:, lanes]
                b_ref[pl.ds(g * CHUNK_PITCH, rows), :] = b[:, lanes]

    def latent_chunk(k):
        s = pl.multiple_of(k * LRU_CHUNK, LRU_CHUNK)
        s_lo = pl.multiple_of(jnp.maximum(s - HALO, 0), HALO)
        s_hi = pl.multiple_of(jnp.minimum(s + LRU_CHUNK, seq_len - HALO), HALO)
        lo = xa_ref[0, pl.ds(s_lo, HALO), :].astype(F32)
        hi = xa_ref[0, pl.ds(s_hi, HALO), :].astype(F32)
        lo = jnp.where(k > 0, lo, 0.0)
        hi = jnp.where(k < n_chunks - 1, hi, 0.0)
        mid = xa_ref[0, pl.ds(s, LRU_CHUNK), :].astype(F32)
        return conv(jnp.concatenate([lo, mid, hi], axis=0), LRU_CHUNK)

    def load_step(refs, row):
        a_ref, b_ref = refs
        return (a_ref[pl.ds(row, LRU_GROUPS, stride=CHUNK_PITCH), :],
                b_ref[pl.ds(row, LRU_GROUPS, stride=CHUNK_PITCH), :])

    def scan_steps(cset, carry, steps, store):
        hf, hr = carry
        for j in range(steps):
            jr = steps - 1 - j
            a, b = load_step(coef_sets[cset][0], j)
            hf = a * hf + b
            a, b = load_step(coef_sets[cset][1], jr)
            hr = a * hr + b
            if store is not None:
                store(j, jr, hf, hr)
        return hf, hr

    def prepare_from_conv(k, cset):
        kr = n_chunks - 1 - k
        uf = latent_chunk(k)
        u_ref[pl.ds(pl.multiple_of(k * LRU_CHUNK, LRU_CHUNK), LRU_CHUNK), :] = uf
        coefficients(uf, 0, LRU_CHUNK, cset)
        ur = latent_chunk(kr)
        u_ref[pl.ds(pl.multiple_of(kr * LRU_CHUNK, LRU_CHUNK), LRU_CHUNK), :] = ur
        coefficients(ur, 1, LRU_CHUNK, cset)

    def prepare_from_cache(k, cset):
        kr = n_chunks - 1 - k
        coefficients(u_ref[pl.ds(pl.multiple_of(k * LRU_CHUNK, LRU_CHUNK), LRU_CHUNK), :],
                     0, LRU_CHUNK, cset)
        coefficients(u_ref[pl.ds(pl.multiple_of(kr * LRU_CHUNK, LRU_CHUNK), LRU_CHUNK), :],
                     1, LRU_CHUNK, cset)

    zpad = jnp.zeros((HALO, LRU_TILE), F32)
    u_ctx = conv(jnp.concatenate([zpad, xc_ref[0].astype(F32), zpad], axis=0), ctx_len)
    coefficients(u_ctx, 0, ctx_len, 1)
    coefficients(u_ctx, 1, ctx_len, 1)
    prepare_from_conv(jnp.int32(0), 0)
    h0 = jnp.zeros((LRU_GROUPS, LANES), F32)
    carry = scan_steps(1, (h0, h0), ctx_len, None)

    def scan_first_half(k, cset, carry):
        sf = k * LRU_CHUNK
        sr = (n_chunks - 1 - k) * LRU_CHUNK

        def store(j, jr, hf, hr):
            y_ref[pl.ds(sf + j, LRU_GROUPS, stride=y_pitch), :] = hf
            y_ref[pl.ds(sr + jr, LRU_GROUPS, stride=y_pitch), :] = hr

        return scan_steps(cset, carry, LRU_CHUNK, store)

    def scan_second_half(k, cset, carry):
        def store(j, jr, hf, hr):
            hf_ref[pl.ds(j, LRU_GROUPS, stride=CHUNK_PITCH), :] = hf
            hr_ref[pl.ds(jr, LRU_GROUPS, stride=CHUNK_PITCH), :] = hr

        carry = scan_steps(cset, carry, LRU_CHUNK, store)
        for kk, h_ref in ((k, hf_ref), (n_chunks - 1 - k, hr_ref)):
            s = pl.multiple_of(kk * LRU_CHUNK, LRU_CHUNK)
            y = jnp.concatenate(
                [y_ref[pl.ds(g * y_pitch + s, LRU_CHUNK), :]
                 + h_ref[pl.ds(g * CHUNK_PITCH, LRU_CHUNK), :] for g in range(LRU_GROUPS)],
                axis=1)
            gate = ga_ref[0, pl.ds(s, LRU_CHUNK), :].astype(F32)
            o_ref[0, pl.ds(s, LRU_CHUNK), :] = (y * _silu(gate)).astype(o_ref.dtype)
        return carry

    def first_half(m, carry):
        k = 2 * m
        prepare_from_conv(k + 1, 1)
        carry = scan_first_half(k, 0, carry)
        prepare_from_conv(k + 2, 0)
        return scan_first_half(k + 1, 1, carry)

    def second_half(m, carry):
        k = 2 * m
        prepare_from_cache(k + 1, 1)
        carry = scan_second_half(k, 0, carry)
        prepare_from_cache(jnp.minimum(k + 2, n_chunks - 1), 0)
        return scan_second_half(k + 1, 1, carry)

    quarter = n_chunks // 4
    carry = lax.fori_loop(0, quarter, first_half, carry)
    lax.fori_loop(quarter, 2 * quarter, second_half, carry)


def _lru(proj, proj_c, conv_w, conv_b, lam, wg, b_r, b_i):
    bsz, seq_len, _ = proj.shape
    ctx_len = proj_c.shape[1]
    assert seq_len % (4 * LRU_CHUNK) == 0 and ctx_len <= LRU_CHUNK and ctx_len % HALO == 0
    n_tiles = LRU_W // LRU_TILE
    heads = LRU_TILE // LRU_HEAD_DIM
    gate_block0 = (MIX_W) // LRU_TILE
    slab = pltpu.VMEM((SLAB_SET, LANES), F32)
    kern = functools.partial(_lru_kernel, seq_len=seq_len, ctx_len=ctx_len)
    return pl.pallas_call(
        kern,
        grid=(bsz, n_tiles),
        in_specs=[
            pl.BlockSpec((1, seq_len, LRU_TILE), lambda b, t: (b, 0, t)),
            pl.BlockSpec((1, seq_len, LRU_TILE), lambda b, t: (b, 0, gate_block0 + t)),
            pl.BlockSpec((1, ctx_len, LRU_TILE), lambda b, t: (b, 0, t)),
            pl.BlockSpec((CONV_W, LRU_TILE), lambda b, t: (0, t)),
            pl.BlockSpec((1, LRU_TILE), lambda b, t: (0, t)),
            pl.BlockSpec((2, LRU_TILE), lambda b, t: (0, t)),
            pl.BlockSpec((2, heads, LRU_HEAD_DIM, 2 * LRU_HEAD_DIM), lambda b, t: (0, t, 0, 0)),
            pl.BlockSpec((2, LRU_TILE), lambda b, t: (0, t)),
            pl.BlockSpec((2, LRU_TILE), lambda b, t: (0, t)),
        ],
        out_specs=pl.BlockSpec((1, seq_len, LRU_TILE), lambda b, t: (b, 0, t)),
        out_shape=jax.ShapeDtypeStruct((bsz, seq_len, LRU_W), BF16),
        scratch_shapes=[slab] * 10 + [
                        pltpu.VMEM((LRU_GROUPS * (seq_len + 8), LANES), F32),
                        pltpu.VMEM((seq_len, LRU_TILE), F32)],
        compiler_params=pltpu.CompilerParams(
            dimension_semantics=("parallel", "parallel"), vmem_limit_bytes=VMEM_LIMIT_BYTES),
        name="lru",
    )(proj, proj, proj_c, conv_w, conv_b.reshape(1, LRU_W), lam, wg, b_r, b_i)


POOL_ROWS = 1024
POOL_BAND = 256


def _pool_operators():
    t = np.arange(POOL_BAND)
    seg, pos = t // GRID_W, t % GRID_W
    wins, inv = [], []
    for w in POOL_WINDOWS:
        left = w // 2
        right = w - 1 - left
        lo = np.maximum(pos - left, 0)
        hi = np.minimum(pos + right, GRID_W - 1)
        member = ((seg[:, None] == seg[None, :])
                  & (pos[None, :] >= lo[:, None]) & (pos[None, :] <= hi[:, None]))
        wins.append(member.astype(np.float32))
        inv.append(np.broadcast_to((1.0 / (hi - lo + 1).astype(np.float32))[:, None],
                                   (POOL_BAND, LANES)))
    return jnp.asarray(np.stack(wins), BF16), jnp.asarray(np.stack(inv), F32)


def _pool_kernel(xb_ref, gb_ref, win_ref, inv_ref, wp_ref, bp_ref, sc_ref, o_ref):
    inv = jnp.tile(inv_ref[0], (1, POOL_GROUP_DIM // LANES))
    zs = []
    for sub in range(POOL_ROWS // POOL_BAND):
        xs = xb_ref[0, sub * POOL_BAND:(sub + 1) * POOL_BAND, :]
        sums = jnp.dot(win_ref[0], xs, preferred_element_type=F32)
        zs.append((sums * inv - xs.astype(F32)).astype(BF16))
    z = jnp.concatenate(zs, axis=0)
    y = (jnp.dot(z, wp_ref[0], preferred_element_type=F32) + bp_ref[...]) * sc_ref[...]
    o_ref[0] = (y * _silu(gb_ref[0].astype(F32))).astype(o_ref.dtype)


def _pool(proj, wp_bf16, b_pool, pool_scale):
    bsz, seq_len, _ = proj.shape
    assert seq_len % POOL_ROWS == 0 and POOL_BAND % GRID_W == 0
    groups = len(POOL_WINDOWS)
    xb_block0 = LRU_W // POOL_GROUP_DIM
    gb_block0 = (MIX_W + LRU_W) // POOL_GROUP_DIM
    win, inv = _pool_operators()
    return pl.pallas_call(
        _pool_kernel,
        grid=(groups, bsz, seq_len // POOL_ROWS),
        in_specs=[
            pl.BlockSpec((1, POOL_ROWS, POOL_GROUP_DIM), lambda g, b, l: (b, l, xb_block0 + g)),
            pl.BlockSpec((1, POOL_ROWS, POOL_GROUP_DIM), lambda g, b, l: (b, l, gb_block0 + g)),
            pl.BlockSpec((1, POOL_BAND, POOL_BAND), lambda g, b, l: (g, 0, 0)),
            pl.BlockSpec((1, POOL_BAND, LANES), lambda g, b, l: (g, 0, 0)),
            pl.BlockSpec((1, POOL_GROUP_DIM, POOL_GROUP_DIM), lambda g, b, l: (g, 0, 0)),
            pl.BlockSpec((1, POOL_GROUP_DIM), lambda g, b, l: (0, g)),
            pl.BlockSpec((1, POOL_GROUP_DIM), lambda g, b, l: (0, g)),
        ],
        out_specs=pl.BlockSpec((1, POOL_ROWS, POOL_GROUP_DIM), lambda g, b, l: (b, l, g)),
        out_shape=jax.ShapeDtypeStruct((bsz, seq_len, POOL_W), BF16),
        compiler_params=pltpu.CompilerParams(
            dimension_semantics=("parallel", "parallel", "parallel"),
            vmem_limit_bytes=VMEM_LIMIT_BYTES),
        name="pool",
    )(proj, proj, win, inv, wp_bf16, b_pool.reshape(1, POOL_W), pool_scale.reshape(1, POOL_W))


OUT_COLS = 512
OUT_DOT_COLS = 1024


def _outproj_kernel(ma_ref, mb_ref, w_ref, x_hbm, gate_ref, gf_ref, o_ref, xbuf, sem, *, k_half):
    i = pl.program_id(0)
    k = pl.program_id(1)
    tm, d = o_ref.shape
    n_chunks = d // OUT_COLS

    def x_copy(c, slot):
        return pltpu.make_async_copy(
            x_hbm.at[pl.ds(i * tm, tm), pl.ds(c * OUT_COLS, OUT_COLS)], xbuf.at[slot], sem.at[slot])

    def accumulate(lhs_ref, first):
        for n in range(d // OUT_DOT_COLS):
            cols = slice(n * OUT_DOT_COLS, (n + 1) * OUT_DOT_COLS)
            part = jnp.dot(lhs_ref[...], w_ref[:, cols], preferred_element_type=F32)
            if first:
                o_ref[:, cols] = part
            else:
                o_ref[:, cols] += part

    @pl.when(k == 0)
    def _():
        x_copy(0, 0).start()
        x_copy(1, 1).start()
        accumulate(ma_ref, True)

    @pl.when((k > 0) & (k < k_half))
    def _():
        accumulate(ma_ref, False)

    @pl.when(k >= k_half)
    def _():
        accumulate(mb_ref, False)

    @pl.when(k == 2 * k_half - 1)
    def _():
        ss = jnp.zeros((tm, 1), F32)
        for c in range(n_chunks):
            slot = c % 2
            cols = slice(c * OUT_COLS, (c + 1) * OUT_COLS)
            x_copy(c, slot).wait()
            xn = xbuf[slot] + gate_ref[0][:, cols] * o_ref[:, cols]
            o_ref[:, cols] = xn
            ss = ss + jnp.sum(xn * xn, axis=-1, keepdims=True)
            if c + 2 < n_chunks:
                x_copy(c + 2, slot).start()
        inv = lax.rsqrt(ss / d + EPS)
        for c in range(n_chunks):
            cols = slice(c * OUT_COLS, (c + 1) * OUT_COLS)
            o_ref[:, cols] = o_ref[:, cols] * inv * gf_ref[:, cols]


def _out_proj(ma, mb, w_bf16, x2d, gate, g_final, rows_per_mod, tm=1024, tk=512):
    m, d = x2d.shape
    k_half = ma.shape[1] // tk
    tiles_per_mod = rows_per_mod // tm
    kern = functools.partial(_outproj_kernel, k_half=k_half)
    return pl.pallas_call(
        kern,
        grid=(m // tm, 2 * k_half),
        in_specs=[
            pl.BlockSpec((tm, tk), lambda i, k: (i, jnp.minimum(k, k_half - 1))),
            pl.BlockSpec((tm, tk), lambda i, k: (i, jnp.maximum(k - k_half, 0))),
            pl.BlockSpec((tk, d), lambda i, k: (k, 0)),
            pl.BlockSpec(memory_space=pl.ANY),
            pl.BlockSpec((1, 1, d), lambda i, k: (i // tiles_per_mod, 0, 0)),
            pl.BlockSpec((1, d), lambda i, k: (0, 0)),
        ],
        out_specs=pl.BlockSpec((tm, d), lambda i, k: (i, 0)),
        out_shape=jax.ShapeDtypeStruct((m, d), F32),
        scratch_shapes=[pltpu.VMEM((2, tm, OUT_COLS), F32), pltpu.SemaphoreType.DMA((2,))],
        compiler_params=pltpu.CompilerParams(
            dimension_semantics=("arbitrary", "arbitrary"), vmem_limit_bytes=VMEM_LIMIT_BYTES),
        name="out_proj",
    )(ma, mb, w_bf16, x2d, gate, g_final.reshape(1, d))


def kernel(x, c, ctx, c_ctx, w_ada, b_ada, g_norm, w_in, conv_w, conv_b, lru_lambda,
           w_rgate, b_rgate, w_igate, b_igate, w_pool, b_pool, pool_scale, w_out, g_final):
    bsz, seq_len, d = x.shape
    ctx_len = ctx.shape[1]
    depth = w_ada.shape[0]
    assert depth == 1 and d == D_MODEL
    l = 0

    cc = jnp.zeros((8, d), F32).at[:bsz].set(c).at[bsz].set(c_ctx)
    mod = _adaln(cc, w_ada[l], b_ada[l])
    shift = mod[:, None, :d]
    scale = mod[:, None, d:2 * d]
    gate = mod[:, None, 2 * d:]

    w_in_b = w_in[l].astype(BF16)
    x2d = x.reshape(bsz * seq_len, d)
    proj = _in_proj(x2d, g_norm[l], shift, scale, w_in_b, seq_len, 2 * MIX_W,
                    ).reshape(bsz, seq_len, 2 * MIX_W)
    proj_c = _in_proj(ctx.reshape(bsz * ctx_len, d), g_norm[l], shift[bsz:bsz + 1],
                      scale[bsz:bsz + 1], w_in_b, bsz * ctx_len, LRU_W,
                      ).reshape(bsz, ctx_len, LRU_W)

    wg = (0.5 * jnp.concatenate([w_rgate[l], w_igate[l]], axis=-1)).astype(BF16)
    mixed_a = _lru(proj, proj_c, conv_w[l], conv_b[l], lru_lambda[l], wg, b_rgate[l], b_igate[l])
    mixed_b = _pool(proj, w_pool[l].astype(BF16), b_pool[l], pool_scale[l])

    out = _out_proj(mixed_a.reshape(bsz * seq_len, LRU_W), mixed_b.reshape(bsz * seq_len, POOL_W),
                    w_out[l].astype(BF16), x2d, gate, g_final, seq_len)
    return out.reshape(bsz, seq_len, d)
```

```python
import functools

import jax
import jax.numpy as jnp
import numpy as np
from jax import lax
from jax.experimental import pallas as pl
from jax.experimental.pallas import tpu as pltpu

D_MODEL = 4096
GRID_W = 64
MIX_W = 2 * D_MODEL
LRU_W = MIX_W // 2
POOL_W = MIX_W - LRU_W
LRU_HEADS = 16
LRU_HEAD_DIM = LRU_W // LRU_HEADS
CONV_W = 4
CONV_LEFT = 1
LRU_C = 8.0
POOL_WINDOWS = (2, 4, 8, 16)
POOL_GROUP_DIM = POOL_W // len(POOL_WINDOWS)
EPS = 1e-6

LANES = 128
VMEM_LIMIT_BYTES = 56 * 1024 * 1024

BF16 = jnp.bfloat16
F32 = jnp.float32


def _sigmoid(v):
    return 0.5 * jnp.tanh(0.5 * v) + 0.5


def _silu(v):
    return v * _sigmoid(v)


def _adaln_kernel(c_ref, w_ref, b_ref, o_ref):
    s = _silu(c_ref[...]).astype(BF16)
    o_ref[...] = jnp.dot(s, w_ref[...].astype(BF16), preferred_element_type=F32) + b_ref[...]


def _adaln(cc, w_ada, b_ada, tn=512):
    rows, d = cc.shape
    n = w_ada.shape[1]
    return pl.pallas_call(
        _adaln_kernel,
        grid=(n // tn,),
        in_specs=[
            pl.BlockSpec((rows, d), lambda j: (0, 0)),
            pl.BlockSpec((d, tn), lambda j: (0, j)),
            pl.BlockSpec((1, tn), lambda j: (0, j)),
        ],
        out_specs=pl.BlockSpec((rows, tn), lambda j: (0, j)),
        out_shape=jax.ShapeDtypeStruct((rows, n), F32),
        compiler_params=pltpu.CompilerParams(
            dimension_semantics=("parallel",), vmem_limit_bytes=VMEM_LIMIT_BYTES),
        name="adaln",
    )(cc, w_ada, b_ada.reshape(1, n))


def _inproj_kernel(x_ref, g_ref, shift_ref, scale_ref, w_ref, o_ref, h_ref, *, n_sub):
    j = pl.program_id(1)
    sub_rows = x_ref.shape[0]

    @pl.when(j < n_sub)
    def _():
        xf = x_ref[...]
        ms = jnp.mean(xf * xf, axis=-1, keepdims=True)
        y = xf * lax.rsqrt(ms + EPS) * g_ref[...]
        h = (y * (1.0 + scale_ref[0]) + shift_ref[0]).astype(BF16)
        h_ref[pl.ds(pl.multiple_of(j * sub_rows, sub_rows), sub_rows), :] = h

    @pl.when(j >= n_sub)
    def _():
        o_ref[...] = jnp.dot(h_ref[...], w_ref[...], preferred_element_type=F32).astype(o_ref.dtype)


def _in_proj(x2d, g_norm, shift, scale, w_bf16, rows_per_mod, n_cols, tm=1024, tn=1024,
             sub_rows=512):
    m, d = x2d.shape
    n_sub = tm // sub_rows
    tiles_per_mod = rows_per_mod // tm
    kern = functools.partial(_inproj_kernel, n_sub=n_sub)
    return pl.pallas_call(
        kern,
        grid=(m // tm, n_sub + n_cols // tn),
        in_specs=[
            pl.BlockSpec((sub_rows, d), lambda i, j: (i * n_sub + jnp.minimum(j, n_sub - 1), 0)),
            pl.BlockSpec((1, d), lambda i, j: (0, 0)),
            pl.BlockSpec((1, 1, d), lambda i, j: (i // tiles_per_mod, 0, 0)),
            pl.BlockSpec((1, 1, d), lambda i, j: (i // tiles_per_mod, 0, 0)),
            pl.BlockSpec((d, tn), lambda i, j: (0, jnp.maximum(j - n_sub, 0))),
        ],
        out_specs=pl.BlockSpec((tm, tn), lambda i, j: (i, jnp.maximum(j - n_sub, 0))),
        out_shape=jax.ShapeDtypeStruct((m, n_cols), BF16),
        scratch_shapes=[pltpu.VMEM((tm, d), BF16)],
        compiler_params=pltpu.CompilerParams(
            dimension_semantics=("parallel", "arbitrary"), vmem_limit_bytes=VMEM_LIMIT_BYTES),
        name="in_proj",
    )(x2d, g_norm.reshape(1, d), shift, scale, w_bf16)


LRU_TILE = 512
LRU_GROUPS = LRU_TILE // LANES
LRU_CHUNK = 256
HALO = 16
CHUNK_PITCH = LRU_CHUNK + 8
SLAB_SET = LRU_GROUPS * CHUNK_PITCH


def _lru_kernel(xa_ref, ga_ref, xc_ref, cw_ref, cb_ref, lam_ref, wg_ref, br_ref, bi_ref,
                o_ref, af0_ref, bf0_ref, ar0_ref, br0_ref, af1_ref, bf1_ref, ar1_ref, br1_ref,
                hf_ref, hr_ref, y_ref, u_ref, *, seq_len, ctx_len):
    n_chunks = seq_len // LRU_CHUNK
    y_pitch = seq_len + 8
    heads = LRU_TILE // LRU_HEAD_DIM
    coef_sets = (((af0_ref, bf0_ref), (ar0_ref, br0_ref)),
                 ((af1_ref, bf1_ref), (ar1_ref, br1_ref)))

    cw = cw_ref[...]
    cb = cb_ref[...]
    half_decay = (-0.5 * LRU_C) * jax.nn.softplus(-lam_ref[...])
    half_br = 0.5 * br_ref[...]
    half_bi = 0.5 * bi_ref[...]

    def conv(x_pad, rows):
        n = rows + 2 * HALO
        u = cb + x_pad[HALO:HALO + rows] * cw[CONV_LEFT:CONV_LEFT + 1]
        for k in range(CONV_W):
            off = k - CONV_LEFT
            if off == 0:
                continue
            u = u + pltpu.roll(x_pad, (-off) % n, axis=0)[HALO:HALO + rows] * cw[k:k + 1]
        return u

    def coefficients(u, d, rows, cset):
        a_ref, b_ref = coef_sets[cset][d]
        for hh in range(heads):
            cols = slice(hh * LRU_HEAD_DIM, (hh + 1) * LRU_HEAD_DIM)
            uh = u[:, cols]
            gates = jnp.dot(uh.astype(BF16), wg_ref[d, hh], preferred_element_type=F32)
            tr = jnp.tanh(gates[:, :LRU_HEAD_DIM] + half_br[d:d + 1, cols])
            ti = jnp.tanh(gates[:, LRU_HEAD_DIM:] + half_bi[d:d + 1, cols])
            i = 0.5 * ti + 0.5
            log_a = half_decay[d:d + 1, cols] * tr + half_decay[d:d + 1, cols]
            a = jnp.exp(log_a)
            t = jnp.tanh(log_a)
            p = -2.0 * t
            q = 1.0 - t
            root = jnp.where(p > 0.0, p * lax.rsqrt(p * q), 0.0)
            b = root * (i * uh)
            for gg in range(LRU_HEAD_DIM // LANES):
                g = hh * (LRU_HEAD_DIM // LANES) + gg
                lanes = slice(gg * LANES, (gg + 1) * LANES)
                a_ref[pl.ds(g * CHUNK_PITCH, rows), :] = a[:, lanes]
                b_ref[pl.ds(g * CHUNK_PITCH, rows), :] = b[:, lanes]

    def latent_chunk(k):
        s = pl.multiple_of(k * LRU_CHUNK, LRU_CHUNK)
        s_lo = pl.multiple_of(jnp.maximum(s - HALO, 0), HALO)
        s_hi = pl.multiple_of(jnp.minimum(s + LRU_CHUNK, seq_len - HALO), HALO)
        lo = xa_ref[0, pl.ds(s_lo, HALO), :].astype(F32)
        hi = xa_ref[0, pl.ds(s_hi, HALO), :].astype(F32)
        lo = jnp.where(k > 0, lo, 0.0)
        hi = jnp.where(k < n_chunks - 1, hi, 0.0)
        mid = xa_ref[0, pl.ds(s, LRU_CHUNK), :].astype(F32)
        return conv(jnp.concatenate([lo, mid, hi], axis=0), LRU_CHUNK)

    def load_step(refs, row):
        a_ref, b_ref = refs
        return (a_ref[pl.ds(row, LRU_GROUPS, stride=CHUNK_PITCH), :],
                b_ref[pl.ds(row, LRU_GROUPS, stride=CHUNK_PITCH), :])

    def scan_steps(cset, carry, steps, store):
        hf, hr = carry
        for j in range(steps):
            jr = steps - 1 - j
            a, b = load_step(coef_sets[cset][0], j)
            hf = a * hf + b
            a, b = load_step(coef_sets[cset][1], jr)
            hr = a * hr + b
            if store is not None:
                store(j, jr, hf, hr)
        return hf, hr

    def prepare_from_conv(k, cset):
        kr = n_chunks - 1 - k
        uf = latent_chunk(k)
        u_ref[pl.ds(pl.multiple_of(k * LRU_CHUNK, LRU_CHUNK), LRU_CHUNK), :] = uf
        coefficients(uf, 0, LRU_CHUNK, cset)
        ur = latent_chunk(kr)
        u_ref[pl.ds(pl.multiple_of(kr * LRU_CHUNK, LRU_CHUNK), LRU_CHUNK), :] = ur
        coefficients(ur, 1, LRU_CHUNK, cset)

    def prepare_from_cache(k, cset):
        kr = n_chunks - 1 - k
        coefficients(u_ref[pl.ds(pl.multiple_of(k * LRU_CHUNK, LRU_CHUNK), LRU_CHUNK), :],
                     0, LRU_CHUNK, cset)
        coefficients(u_ref[pl.ds(pl.multiple_of(kr * LRU_CHUNK, LRU_CHUNK), LRU_CHUNK), :],
                     1, LRU_CHUNK, cset)

    zpad = jnp.zeros((HALO, LRU_TILE), F32)
    u_ctx = conv(jnp.concatenate([zpad, xc_ref[0].astype(F32), zpad], axis=0), ctx_len)
    coefficients(u_ctx, 0, ctx_len, 1)
    coefficients(u_ctx, 1, ctx_len, 1)
    prepare_from_conv(jnp.int32(0), 0)
    h0 = jnp.zeros((LRU_GROUPS, LANES), F32)
    carry = scan_steps(1, (h0, h0), ctx_len, None)

    def scan_first_half(k, cset, carry):
        sf = k * LRU_CHUNK
        sr = (n_chunks - 1 - k) * LRU_CHUNK

        def store(j, jr, hf, hr):
            y_ref[pl.ds(sf + j, LRU_GROUPS, stride=y_pitch), :] = hf
            y_ref[pl.ds(sr + jr, LRU_GROUPS, stride=y_pitch), :] = hr

        return scan_steps(cset, carry, LRU_CHUNK, store)

    def scan_second_half(k, cset, carry):
        def store(j, jr, hf, hr):
            hf_ref[pl.ds(j, LRU_GROUPS, stride=CHUNK_PITCH), :] = hf
            hr_ref[pl.ds(jr, LRU_GROUPS, stride=CHUNK_PITCH), :] = hr

        carry = scan_steps(cset, carry, LRU_CHUNK, store)
        for kk, h_ref in ((k, hf_ref), (n_chunks - 1 - k, hr_ref)):
            s = pl.multiple_of(kk * LRU_CHUNK, LRU_CHUNK)
            y = jnp.concatenate(
                [y_ref[pl.ds(g * y_pitch + s, LRU_CHUNK), :]
                 + h_ref[pl.ds(g * CHUNK_PITCH, LRU_CHUNK), :] for g in range(LRU_GROUPS)],
                axis=1)
            gate = ga_ref[0, pl.ds(s, LRU_CHUNK), :].astype(F32)
            o_ref[0, pl.ds(s, LRU_CHUNK), :] = (y * _silu(gate)).astype(o_ref.dtype)
        return carry

    def first_half(m, carry):
        k = 2 * m
        prepare_from_conv(k + 1, 1)
        carry = scan_first_half(k, 0, carry)
        prepare_from_conv(k + 2, 0)
        return scan_first_half(k + 1, 1, carry)

    def second_half(m, carry):
        k = 2 * m
        prepare_from_cache(k + 1, 1)
        carry = scan_second_half(k, 0, carry)
        prepare_from_cache(jnp.minimum(k + 2, n_chunks - 1), 0)
        return scan_second_half(k + 1, 1, carry)

    quarter = n_chunks // 4
    carry = lax.fori_loop(0, quarter, first_half, carry)
    lax.fori_loop(quarter, 2 * quarter, second_half, carry)


def _lru(proj, proj_c, conv_w, conv_b, lam, wg, b_r, b_i):
    bsz, seq_len, _ = proj.shape
    ctx_len = proj_c.shape[1]
    assert seq_len % (4 * LRU_CHUNK) == 0 and ctx_len <= LRU_CHUNK and ctx_len % HALO == 0
    n_tiles = LRU_W // LRU_TILE
    heads = LRU_TILE // LRU_HEAD_DIM
    gate_block0 = (MIX_W) // LRU_TILE
    slab = pltpu.VMEM((SLAB_SET, LANES), F32)
    kern = functools.partial(_lru_kernel, seq_len=seq_len, ctx_len=ctx_len)
    return pl.pallas_call(
        kern,
        grid=(bsz, n_tiles),
        in_specs=[
            pl.BlockSpec((1, seq_len, LRU_TILE), lambda b, t: (b, 0, t)),
            pl.BlockSpec((1, seq_len, LRU_TILE), lambda b, t: (b, 0, gate_block0 + t)),
            pl.BlockSpec((1, ctx_len, LRU_TILE), lambda b, t: (b, 0, t)),
            pl.BlockSpec((CONV_W, LRU_TILE), lambda b, t: (0, t)),
            pl.BlockSpec((1, LRU_TILE), lambda b, t: (0, t)),
            pl.BlockSpec((2, LRU_TILE), lambda b, t: (0, t)),
            pl.BlockSpec((2, heads, LRU_HEAD_DIM, 2 * LRU_HEAD_DIM), lambda b, t: (0, t, 0, 0)),
            pl.BlockSpec((2, LRU_TILE), lambda b, t: (0, t)),
            pl.BlockSpec((2, LRU_TILE), lambda b, t: (0, t)),
        ],
        out_specs=pl.BlockSpec((1, seq_len, LRU_TILE), lambda b, t: (b, 0, t)),
        out_shape=jax.ShapeDtypeStruct((bsz, seq_len, LRU_W), BF16),
        scratch_shapes=[slab] * 10 + [
                        pltpu.VMEM((LRU_GROUPS * (seq_len + 8), LANES), F32),
                        pltpu.VMEM((seq_len, LRU_TILE), F32)],
        compiler_params=pltpu.CompilerParams(
            dimension_semantics=("parallel", "parallel"), vmem_limit_bytes=VMEM_LIMIT_BYTES),
        name="lru",
    )(proj, proj, proj_c, conv_w, conv_b.reshape(1, LRU_W), lam, wg, b_r, b_i)


POOL_ROWS = 1024
POOL_BAND = 256


def _pool_operators():
    t = np.arange(POOL_BAND)
    seg, pos = t // GRID_W, t % GRID_W
    wins, inv = [], []
    for w in POOL_WINDOWS:
        left = w // 2
        right = w - 1 - left
        lo = np.maximum(pos - left, 0)
        hi = np.minimum(pos + right, GRID_W - 1)
        member = ((seg[:, None] == seg[None, :])
                  & (pos[None, :] >= lo[:, None]) & (pos[None, :] <= hi[:, None]))
        wins.append(member.astype(np.float32))
        inv.append(np.broadcast_to((1.0 / (hi - lo + 1).astype(np.float32))[:, None],
                                   (POOL_BAND, LANES)))
    return jnp.asarray(np.stack(wins), BF16), jnp.asarray(np.stack(inv), F32)


def _pool_kernel(xb_ref, gb_ref, win_ref, inv_ref, wp_ref, bp_ref, sc_ref, o_ref):
    inv = jnp.tile(inv_ref[0], (1, POOL_GROUP_DIM // LANES))
    zs = []
    for sub in range(POOL_ROWS // POOL_BAND):
        xs = xb_ref[0, sub * POOL_BAND:(sub + 1) * POOL_BAND, :]
        sums = jnp.dot(win_ref[0], xs, preferred_element_type=F32)
        zs.append((sums * inv - xs.astype(F32)).astype(BF16))
    z = jnp.concatenate(zs, axis=0)
    y = (jnp.dot(z, wp_ref[0], preferred_element_type=F32) + bp_ref[...]) * sc_ref[...]
    o_ref[0] = (y * _silu(gb_ref[0].astype(F32))).astype(o_ref.dtype)


def _pool(proj, wp_bf16, b_pool, pool_scale):
    bsz, seq_len, _ = proj.shape
    assert seq_len % POOL_ROWS == 0 and POOL_BAND % GRID_W == 0
    groups = len(POOL_WINDOWS)
    xb_block0 = LRU_W // POOL_GROUP_DIM
    gb_block0 = (MIX_W + LRU_W) // POOL_GROUP_DIM
    win, inv = _pool_operators()
    return pl.pallas_call(
        _pool_kernel,
        grid=(groups, bsz, seq_len // POOL_ROWS),
        in_specs=[
            pl.BlockSpec((1, POOL_ROWS, POOL_GROUP_DIM), lambda g, b, l: (b, l, xb_block0 + g)),
            pl.BlockSpec((1, POOL_ROWS, POOL_GROUP_DIM), lambda g, b, l: (b, l, gb_block0 + g)),
            pl.BlockSpec((1, POOL_BAND, POOL_BAND), lambda g, b, l: (g, 0, 0)),
            pl.BlockSpec((1, POOL_BAND, LANES), lambda g, b, l: (g, 0, 0)),
            pl.BlockSpec((1, POOL_GROUP_DIM, POOL_GROUP_DIM), lambda g, b, l: (g, 0, 0)),
            pl.BlockSpec((1, POOL_GROUP_DIM), lambda g, b, l: (0, g)),
            pl.BlockSpec((1, POOL_GROUP_DIM), lambda g, b, l: (0, g)),
        ],
        out_specs=pl.BlockSpec((1, POOL_ROWS, POOL_GROUP_DIM), lambda g, b, l: (b, l, g)),
        out_shape=jax.ShapeDtypeStruct((bsz, seq_len, POOL_W), BF16),
        compiler_params=pltpu.CompilerParams(
            dimension_semantics=("parallel", "parallel", "parallel"),
            vmem_limit_bytes=VMEM_LIMIT_BYTES),
        name="pool",
    )(proj, proj, win, inv, wp_bf16, b_pool.reshape(1, POOL_W), pool_scale.reshape(1, POOL_W))


OUT_COLS = 512


def _outproj_kernel(ma_ref, mb_ref, wa_ref, wb_ref, x_hbm, gate_ref, gf_ref, o_ref, xbuf, sem):
    i = pl.program_id(0)
    j = pl.program_id(1)
    tm, d = o_ref.shape
    n_chunks = d // OUT_COLS

    def x_copy(c, slot):
        return pltpu.make_async_copy(
            x_hbm.at[pl.ds(i * tm, tm), pl.ds(c * OUT_COLS, OUT_COLS)], xbuf.at[slot], sem.at[slot])

    @pl.when(j == 0)
    def _():
        x_copy(0, 0).start()
        x_copy(1, 1).start()

    for jj in range(n_chunks):
        @pl.when(j == jj)
        def _(jj=jj):
            o_ref[:, jj * OUT_COLS:(jj + 1) * OUT_COLS] = (
                jnp.dot(ma_ref[...], wa_ref[...], preferred_element_type=F32)
                + jnp.dot(mb_ref[...], wb_ref[...], preferred_element_type=F32))

    @pl.when(j == n_chunks - 1)
    def _():
        ss = jnp.zeros((tm, 1), F32)
        for c in range(n_chunks):
            slot = c % 2
            cols = slice(c * OUT_COLS, (c + 1) * OUT_COLS)
            x_copy(c, slot).wait()
            xn = xbuf[slot] + gate_ref[0][:, cols] * o_ref[:, cols]
            o_ref[:, cols] = xn
            ss = ss + jnp.sum(xn * xn, axis=-1, keepdims=True)
            if c + 2 < n_chunks:
                x_copy(c + 2, slot).start()
        inv = lax.rsqrt(ss / d + EPS)
        for c in range(n_chunks):
            cols = slice(c * OUT_COLS, (c + 1) * OUT_COLS)
            o_ref[:, cols] = o_ref[:, cols] * inv * gf_ref[:, cols]


def _out_proj(ma, mb, w_bf16, x2d, gate, g_final, rows_per_mod, tm=512):
    m, d = x2d.shape
    k_a = ma.shape[1]
    assert w_bf16.shape[0] == 2 * k_a and mb.shape[1] == k_a
    tiles_per_mod = rows_per_mod // tm
    return pl.pallas_call(
        _outproj_kernel,
        grid=(m // tm, d // OUT_COLS),
        in_specs=[
            pl.BlockSpec((tm, k_a), lambda i, j: (i, 0)),
            pl.BlockSpec((tm, k_a), lambda i, j: (i, 0)),
            pl.BlockSpec((k_a, OUT_COLS), lambda i, j: (0, j)),
            pl.BlockSpec((k_a, OUT_COLS), lambda i, j: (1, j)),
            pl.BlockSpec(memory_space=pl.ANY),
            pl.BlockSpec((1, 1, d), lambda i, j: (i // tiles_per_mod, 0, 0)),
            pl.BlockSpec((1, d), lambda i, j: (0, 0)),
        ],
        out_specs=pl.BlockSpec((tm, d), lambda i, j: (i, 0)),
        out_shape=jax.ShapeDtypeStruct((m, d), F32),
        scratch_shapes=[pltpu.VMEM((2, tm, OUT_COLS), F32), pltpu.SemaphoreType.DMA((2,))],
        compiler_params=pltpu.CompilerParams(
            dimension_semantics=("arbitrary", "arbitrary"), vmem_limit_bytes=VMEM_LIMIT_BYTES),
        name="out_proj",
    )(ma, mb, w_bf16, w_bf16, x2d, gate, g_final.reshape(1, d))


def kernel(x, c, ctx, c_ctx, w_ada, b_ada, g_norm, w_in, conv_w, conv_b, lru_lambda,
           w_rgate, b_rgate, w_igate, b_igate, w_pool, b_pool, pool_scale, w_out, g_final):
    bsz, seq_len, d = x.shape
    ctx_len = ctx.shape[1]
    depth = w_ada.shape[0]
    assert depth == 1 and d == D_MODEL
    l = 0

    cc = jnp.zeros((8, d), F32).at[:bsz].set(c).at[bsz].set(c_ctx)
    mod = _adaln(cc, w_ada[l], b_ada[l])
    shift = mod[:, None, :d]
    scale = mod[:, None, d:2 * d]
    gate = mod[:, None, 2 * d:]

    w_in_b = w_in[l].astype(BF16)
    x2d = x.reshape(bsz * seq_len, d)
    proj = _in_proj(x2d, g_norm[l], shift, scale, w_in_b, seq_len, 2 * MIX_W,
                    ).reshape(bsz, seq_len, 2 * MIX_W)
    proj_c = _in_proj(ctx.reshape(bsz * ctx_len, d), g_norm[l], shift[bsz:bsz + 1],
                      scale[bsz:bsz + 1], w_in_b, bsz * ctx_len, LRU_W,
                      ).reshape(bsz, ctx_len, LRU_W)

    wg = (0.5 * jnp.concatenate([w_rgate[l], w_igate[l]], axis=-1)).astype(BF16)
    mixed_a = _lru(proj, proj_c, conv_w[l], conv_b[l], lru_lambda[l], wg, b_rgate[l], b_igate[l])
    mixed_b = _pool(proj, w_pool[l].astype(BF16), b_pool[l], pool_scale[l])

    out = _out_proj(mixed_a.reshape(bsz * seq_len, LRU_W), mixed_b.reshape(bsz * seq_len, POOL_W),
                    w_out[l].astype(BF16), x2d, gate, g_final, seq_len)
    return out.reshape(bsz, seq_len, d)
```

```python
import functools

import jax
import jax.numpy as jnp
import numpy as np
from jax import lax
from jax.experimental import pallas as pl
from jax.experimental.pallas import tpu as pltpu

D_MODEL = 4096
GRID_W = 64
MIX_W = 2 * D_MODEL
LRU_W = MIX_W // 2
POOL_W = MIX_W - LRU_W
LRU_HEADS = 16
LRU_HEAD_DIM = LRU_W // LRU_HEADS
CONV_W = 4
CONV_LEFT = 1
LRU_C = 8.0
POOL_WINDOWS = (2, 4, 8, 16)
POOL_GROUP_DIM = POOL_W // len(POOL_WINDOWS)
EPS = 1e-6

LANES = 128
VMEM_LIMIT_BYTES = 56 * 1024 * 1024

BF16 = jnp.bfloat16
F32 = jnp.float32


def _sigmoid(v):
    return 0.5 * jnp.tanh(0.5 * v) + 0.5


def _silu(v):
    return v * _sigmoid(v)


def _adaln_kernel(c_ref, w_ref, b_ref, o_ref):
    s = _silu(c_ref[...]).astype(BF16)
    o_ref[...] = jnp.dot(s, w_ref[...].astype(BF16), preferred_element_type=F32) + b_ref[...]


def _adaln(cc, w_ada, b_ada, tn=512):
    rows, d = cc.shape
    n = w_ada.shape[1]
    return pl.pallas_call(
        _adaln_kernel,
        grid=(n // tn,),
        in_specs=[
            pl.BlockSpec((rows, d), lambda j: (0, 0)),
            pl.BlockSpec((d, tn), lambda j: (0, j)),
            pl.BlockSpec((1, tn), lambda j: (0, j)),
        ],
        out_specs=pl.BlockSpec((rows, tn), lambda j: (0, j)),
        out_shape=jax.ShapeDtypeStruct((rows, n), F32),
        compiler_params=pltpu.CompilerParams(
            dimension_semantics=("parallel",), vmem_limit_bytes=VMEM_LIMIT_BYTES),
        name="adaln",
    )(cc, w_ada, b_ada.reshape(1, n))


def _inproj_kernel(x_ref, g_ref, shift_ref, scale_ref, w_ref, *rest, n_sub, has_side):
    if has_side:
        side_ref, o_ref, side_out_ref, h_ref = rest
    else:
        o_ref, h_ref = rest
    j = pl.program_id(1)
    sub_rows = x_ref.shape[0]

    @pl.when(j < n_sub)
    def _():
        xf = x_ref[...]
        ms = jnp.mean(xf * xf, axis=-1, keepdims=True)
        y = xf * lax.rsqrt(ms + EPS) * g_ref[...]
        h = (y * (1.0 + scale_ref[0]) + shift_ref[0]).astype(BF16)
        h_ref[pl.ds(pl.multiple_of(j * sub_rows, sub_rows), sub_rows), :] = h

    @pl.when(j >= n_sub)
    def _():
        o_ref[...] = jnp.dot(h_ref[...], w_ref[...], preferred_element_type=F32).astype(o_ref.dtype)
        if has_side:
            side_out_ref[...] = side_ref[...].astype(BF16)


def _in_proj(x2d, g_norm, shift, scale, w_bf16, rows_per_mod, n_cols, side=None, tm=1024, tn=1024,
             sub_rows=512):
    m, d = x2d.shape
    n_sub = tm // sub_rows
    n_mm = n_cols // tn
    tiles_per_mod = rows_per_mod // tm
    kern = functools.partial(_inproj_kernel, n_sub=n_sub, has_side=side is not None)

    def mm_step(j):
        return jnp.maximum(j - n_sub, 0)

    in_specs = [
        pl.BlockSpec((sub_rows, d), lambda i, j: (i * n_sub + jnp.minimum(j, n_sub - 1), 0)),
        pl.BlockSpec((1, d), lambda i, j: (0, 0)),
        pl.BlockSpec((1, 1, d), lambda i, j: (i // tiles_per_mod, 0, 0)),
        pl.BlockSpec((1, 1, d), lambda i, j: (i // tiles_per_mod, 0, 0)),
        pl.BlockSpec((d, tn), lambda i, j: (0, mm_step(j))),
    ]
    out_specs = pl.BlockSpec((tm, tn), lambda i, j: (i, mm_step(j)))
    out_shape = jax.ShapeDtypeStruct((m, n_cols), BF16)
    args = (x2d, g_norm.reshape(1, d), shift, scale, w_bf16)
    if side is not None:
        side_rows, side_cols = side.shape
        steps = (m // tm) * n_mm
        assert side_rows % (16 * steps) == 0
        side_spec = pl.BlockSpec((side_rows // steps, side_cols),
                                 lambda i, j: (i * n_mm + mm_step(j), 0))
        in_specs.append(side_spec)
        out_specs = (out_specs, side_spec)
        out_shape = (out_shape, jax.ShapeDtypeStruct(side.shape, BF16))
        args = args + (side,)
    return pl.pallas_call(
        kern,
        grid=(m // tm, n_sub + n_mm),
        in_specs=in_specs,
        out_specs=out_specs,
        out_shape=out_shape,
        scratch_shapes=[pltpu.VMEM((tm, d), BF16)],
        compiler_params=pltpu.CompilerParams(
            dimension_semantics=("parallel", "arbitrary"), vmem_limit_bytes=VMEM_LIMIT_BYTES),
        name="in_proj",
    )(*args)


LRU_TILE = 512
LRU_GROUPS = LRU_TILE // LANES
LRU_CHUNK = 256
HALO = 16
CHUNK_PITCH = LRU_CHUNK + 8
SLAB_SET = LRU_GROUPS * CHUNK_PITCH


def _lru_kernel(xa_ref, ga_ref, xc_ref, cw_ref, cb_ref, lam_ref, wg_ref, br_ref, bi_ref,
                o_ref, af0_ref, bf0_ref, ar0_ref, br0_ref, af1_ref, bf1_ref, ar1_ref, br1_ref,
                hf_ref, hr_ref, y_ref, u_ref, *, seq_len, ctx_len):
    n_chunks = seq_len // LRU_CHUNK
    y_pitch = seq_len + 8
    heads = LRU_TILE // LRU_HEAD_DIM
    coef_sets = (((af0_ref, bf0_ref), (ar0_ref, br0_ref)),
                 ((af1_ref, bf1_ref), (ar1_ref, br1_ref)))

    cw = cw_ref[...]
    cb = cb_ref[...]
    half_decay = (-0.5 * LRU_C) * jax.nn.softplus(-lam_ref[...])
    half_br = 0.5 * br_ref[...]
    half_bi = 0.5 * bi_ref[...]

    def conv(x_pad, rows):
        n = rows + 2 * HALO
        u = cb + x_pad[HALO:HALO + rows] * cw[CONV_LEFT:CONV_LEFT + 1]
        for k in range(CONV_W):
            off = k - CONV_LEFT
            if off == 0:
                continue
            u = u + pltpu.roll(x_pad, (-off) % n, axis=0)[HALO:HALO + rows] * cw[k:k + 1]
        return u

    def coefficients(u, d, rows, cset):
        a_ref, b_ref = coef_sets[cset][d]
        for hh in range(heads):
            cols = slice(hh * LRU_HEAD_DIM, (hh + 1) * LRU_HEAD_DIM)
            uh = u[:, cols]
            gates = jnp.dot(uh.astype(BF16), wg_ref[d, hh], preferred_element_type=F32)
            tr = jnp.tanh(gates[:, :LRU_HEAD_DIM] + half_br[d:d + 1, cols])
            ti = jnp.tanh(gates[:, LRU_HEAD_DIM:] + half_bi[d:d + 1, cols])
            i = 0.5 * ti + 0.5
            log_a = half_decay[d:d + 1, cols] * tr + half_decay[d:d + 1, cols]
            a = jnp.exp(log_a)
            t = jnp.tanh(log_a)
            p = -2.0 * t
            q = 1.0 - t
            root = jnp.where(p > 0.0, p * lax.rsqrt(p * q), 0.0)
            b = root * (i * uh)
            for gg in range(LRU_HEAD_DIM // LANES):
                g = hh * (LRU_HEAD_DIM // LANES) + gg
                lanes = slice(gg * LANES, (gg + 1) * LANES)
                a_ref[pl.ds(g * CHUNK_PITCH, rows), :] = a[:, lanes]
                b_ref[pl.ds(g * CHUNK_PITCH, rows), :] = b[:, lanes]

    def latent_chunk(k):
        s = pl.multiple_of(k * LRU_CHUNK, LRU_CHUNK)
        s_lo = pl.multiple_of(jnp.maximum(s - HALO, 0), HALO)
        s_hi = pl.multiple_of(jnp.minimum(s + LRU_CHUNK, seq_len - HALO), HALO)
        lo = xa_ref[0, pl.ds(s_lo, HALO), :].astype(F32)
        hi = xa_ref[0, pl.ds(s_hi, HALO), :].astype(F32)
        lo = jnp.where(k > 0, lo, 0.0)
        hi = jnp.where(k < n_chunks - 1, hi, 0.0)
        mid = xa_ref[0, pl.ds(s, LRU_CHUNK), :].astype(F32)
        return conv(jnp.concatenate([lo, mid, hi], axis=0), LRU_CHUNK)

    def load_step(refs, row):
        a_ref, b_ref = refs
        return (a_ref[pl.ds(row, LRU_GROUPS, stride=CHUNK_PITCH), :],
                b_ref[pl.ds(row, LRU_GROUPS, stride=CHUNK_PITCH), :])

    def scan_steps(cset, carry, steps, store):
        hf, hr = carry
        for j in range(steps):
            jr = steps - 1 - j
            a, b = load_step(coef_sets[cset][0], j)
            hf = a * hf + b
            a, b = load_step(coef_sets[cset][1], jr)
            hr = a * hr + b
            if store is not None:
                store(j, jr, hf, hr)
        return hf, hr

    def prepare_from_conv(k, cset):
        kr = n_chunks - 1 - k
        uf = latent_chunk(k)
        u_ref[pl.ds(pl.multiple_of(k * LRU_CHUNK, LRU_CHUNK), LRU_CHUNK), :] = uf
        coefficients(uf, 0, LRU_CHUNK, cset)
        ur = latent_chunk(kr)
        u_ref[pl.ds(pl.multiple_of(kr * LRU_CHUNK, LRU_CHUNK), LRU_CHUNK), :] = ur
        coefficients(ur, 1, LRU_CHUNK, cset)

    def prepare_from_cache(k, cset):
        kr = n_chunks - 1 - k
        coefficients(u_ref[pl.ds(pl.multiple_of(k * LRU_CHUNK, LRU_CHUNK), LRU_CHUNK), :],
                     0, LRU_CHUNK, cset)
        coefficients(u_ref[pl.ds(pl.multiple_of(kr * LRU_CHUNK, LRU_CHUNK), LRU_CHUNK), :],
                     1, LRU_CHUNK, cset)

    zpad = jnp.zeros((HALO, LRU_TILE), F32)
    u_ctx = conv(jnp.concatenate([zpad, xc_ref[0].astype(F32), zpad], axis=0), ctx_len)
    coefficients(u_ctx, 0, ctx_len, 1)
    coefficients(u_ctx, 1, ctx_len, 1)
    prepare_from_conv(jnp.int32(0), 0)
    h0 = jnp.zeros((LRU_GROUPS, LANES), F32)
    carry = scan_steps(1, (h0, h0), ctx_len, None)

    def scan_first_half(k, cset, carry):
        sf = k * LRU_CHUNK
        sr = (n_chunks - 1 - k) * LRU_CHUNK

        def store(j, jr, hf, hr):
            y_ref[pl.ds(sf + j, LRU_GROUPS, stride=y_pitch), :] = hf
            y_ref[pl.ds(sr + jr, LRU_GROUPS, stride=y_pitch), :] = hr

        return scan_steps(cset, carry, LRU_CHUNK, store)

    def scan_second_half(k, cset, carry):
        def store(j, jr, hf, hr):
            hf_ref[pl.ds(j, LRU_GROUPS, stride=CHUNK_PITCH), :] = hf
            hr_ref[pl.ds(jr, LRU_GROUPS, stride=CHUNK_PITCH), :] = hr

        carry = scan_steps(cset, carry, LRU_CHUNK, store)
        for kk, h_ref in ((k, hf_ref), (n_chunks - 1 - k, hr_ref)):
            s = pl.multiple_of(kk * LRU_CHUNK, LRU_CHUNK)
            y = jnp.concatenate(
                [y_ref[pl.ds(g * y_pitch + s, LRU_CHUNK), :]
                 + h_ref[pl.ds(g * CHUNK_PITCH, LRU_CHUNK), :] for g in range(LRU_GROUPS)],
                axis=1)
            gate = ga_ref[0, pl.ds(s, LRU_CHUNK), :].astype(F32)
            o_ref[0, pl.ds(s, LRU_CHUNK), :] = (y * _silu(gate)).astype(o_ref.dtype)
        return carry

    def first_half(m, carry):
        k = 2 * m
        prepare_from_conv(k + 1, 1)
        carry = scan_first_half(k, 0, carry)
        prepare_from_conv(k + 2, 0)
        return scan_first_half(k + 1, 1, carry)

    def second_half(m, carry):
        k = 2 * m
        prepare_from_cache(k + 1, 1)
        carry = scan_second_half(k, 0, carry)
        prepare_from_cache(jnp.minimum(k + 2, n_chunks - 1), 0)
        return scan_second_half(k + 1, 1, carry)

    quarter = n_chunks // 4
    carry = lax.fori_loop(0, quarter, first_half, carry)
    lax.fori_loop(quarter, 2 * quarter, second_half, carry)


def _lru(proj, proj_c, conv_w, conv_b, lam, wg, b_r, b_i):
    bsz, seq_len, _ = proj.shape
    ctx_len = proj_c.shape[1]
    assert seq_len % (4 * LRU_CHUNK) == 0 and ctx_len <= LRU_CHUNK and ctx_len % HALO == 0
    n_tiles = LRU_W // LRU_TILE
    heads = LRU_TILE // LRU_HEAD_DIM
    gate_block0 = (MIX_W) // LRU_TILE
    slab = pltpu.VMEM((SLAB_SET, LANES), F32)
    kern = functools.partial(_lru_kernel, seq_len=seq_len, ctx_len=ctx_len)
    return pl.pallas_call(
        kern,
        grid=(bsz, n_tiles),
        in_specs=[
            pl.BlockSpec((1, seq_len, LRU_TILE), lambda b, t: (b, 0, t)),
            pl.BlockSpec((1, seq_len, LRU_TILE), lambda b, t: (b, 0, gate_block0 + t)),
            pl.BlockSpec((1, ctx_len, LRU_TILE), lambda b, t: (b, 0, t)),
            pl.BlockSpec((CONV_W, LRU_TILE), lambda b, t: (0, t)),
            pl.BlockSpec((1, LRU_TILE), lambda b, t: (0, t)),
            pl.BlockSpec((2, LRU_TILE), lambda b, t: (0, t)),
            pl.BlockSpec((2, heads, LRU_HEAD_DIM, 2 * LRU_HEAD_DIM), lambda b, t: (0, t, 0, 0)),
            pl.BlockSpec((2, LRU_TILE), lambda b, t: (0, t)),
            pl.BlockSpec((2, LRU_TILE), lambda b, t: (0, t)),
        ],
        out_specs=pl.BlockSpec((1, seq_len, LRU_TILE), lambda b, t: (b, 0, t)),
        out_shape=jax.ShapeDtypeStruct((bsz, seq_len, LRU_W), BF16),
        scratch_shapes=[slab] * 10 + [
                        pltpu.VMEM((LRU_GROUPS * (seq_len + 8), LANES), F32),
                        pltpu.VMEM((seq_len, LRU_TILE), F32)],
        compiler_params=pltpu.CompilerParams(
            dimension_semantics=("parallel", "parallel"), vmem_limit_bytes=VMEM_LIMIT_BYTES),
        name="lru",
    )(proj, proj, proj_c, conv_w, conv_b.reshape(1, LRU_W), lam, wg, b_r, b_i)


POOL_ROWS = 1024
POOL_BAND = 256


def _pool_operators():
    t = np.arange(POOL_BAND)
    seg, pos = t // GRID_W, t % GRID_W
    wins, inv = [], []
    for w in POOL_WINDOWS:
        left = w // 2
        right = w - 1 - left
        lo = np.maximum(pos - left, 0)
        hi = np.minimum(pos + right, GRID_W - 1)
        member = ((seg[:, None] == seg[None, :])
                  & (pos[None, :] >= lo[:, None]) & (pos[None, :] <= hi[:, None]))
        wins.append(member.astype(np.float32))
        inv.append(np.broadcast_to((1.0 / (hi - lo + 1).astype(np.float32))[:, None],
                                   (POOL_BAND, LANES)))
    return jnp.asarray(np.stack(wins), BF16), jnp.asarray(np.stack(inv), F32)


def _pool_kernel(xb_ref, gb_ref, win_ref, inv_ref, wp_ref, bp_ref, sc_ref, o_ref):
    inv = jnp.tile(inv_ref[0], (1, POOL_GROUP_DIM // LANES))
    zs = []
    for sub in range(POOL_ROWS // POOL_BAND):
        xs = xb_ref[0, sub * POOL_BAND:(sub + 1) * POOL_BAND, :]
        sums = jnp.dot(win_ref[0], xs, preferred_element_type=F32)
        zs.append((sums * inv - xs.astype(F32)).astype(BF16))
    z = jnp.concatenate(zs, axis=0)
    y = (jnp.dot(z, wp_ref[0], preferred_element_type=F32) + bp_ref[...]) * sc_ref[...]
    o_ref[0] = (y * _silu(gb_ref[0].astype(F32))).astype(o_ref.dtype)


def _pool(proj, wp_bf16, b_pool, pool_scale):
    bsz, seq_len, _ = proj.shape
    assert seq_len % POOL_ROWS == 0 and POOL_BAND % GRID_W == 0
    groups = len(POOL_WINDOWS)
    xb_block0 = LRU_W // POOL_GROUP_DIM
    gb_block0 = (MIX_W + LRU_W) // POOL_GROUP_DIM
    win, inv = _pool_operators()
    return pl.pallas_call(
        _pool_kernel,
        grid=(groups, bsz, seq_len // POOL_ROWS),
        in_specs=[
            pl.BlockSpec((1, POOL_ROWS, POOL_GROUP_DIM), lambda g, b, l: (b, l, xb_block0 + g)),
            pl.BlockSpec((1, POOL_ROWS, POOL_GROUP_DIM), lambda g, b, l: (b, l, gb_block0 + g)),
            pl.BlockSpec((1, POOL_BAND, POOL_BAND), lambda g, b, l: (g, 0, 0)),
            pl.BlockSpec((1, POOL_BAND, LANES), lambda g, b, l: (g, 0, 0)),
            pl.BlockSpec((1, POOL_GROUP_DIM, POOL_GROUP_DIM), lambda g, b, l: (g, 0, 0)),
            pl.BlockSpec((1, POOL_GROUP_DIM), lambda g, b, l: (0, g)),
            pl.BlockSpec((1, POOL_GROUP_DIM), lambda g, b, l: (0, g)),
        ],
        out_specs=pl.BlockSpec((1, POOL_ROWS, POOL_GROUP_DIM), lambda g, b, l: (b, l, g)),
        out_shape=jax.ShapeDtypeStruct((bsz, seq_len, POOL_W), BF16),
        compiler_params=pltpu.CompilerParams(
            dimension_semantics=("parallel", "parallel", "parallel"),
            vmem_limit_bytes=VMEM_LIMIT_BYTES),
        name="pool",
    )(proj, proj, win, inv, wp_bf16, b_pool.reshape(1, POOL_W), pool_scale.reshape(1, POOL_W))


OUT_COLS = 512
OUT_DOT_COLS = 1024


def _outproj_kernel(ma_ref, mb_ref, w_ref, x_hbm, gate_ref, gf_ref, o_ref, xbuf, sem, *, k_half):
    i = pl.program_id(0)
    k = pl.program_id(1)
    tm, d = o_ref.shape
    n_chunks = d // OUT_COLS

    def x_copy(c, slot):
        return pltpu.make_async_copy(
            x_hbm.at[pl.ds(i * tm, tm), pl.ds(c * OUT_COLS, OUT_COLS)], xbuf.at[slot], sem.at[slot])

    def accumulate(lhs_ref, first):
        for n in range(d // OUT_DOT_COLS):
            cols = slice(n * OUT_DOT_COLS, (n + 1) * OUT_DOT_COLS)
            part = jnp.dot(lhs_ref[...], w_ref[:, cols], preferred_element_type=F32)
            if first:
                o_ref[:, cols] = part
            else:
                o_ref[:, cols] += part

    @pl.when(k == 0)
    def _():
        x_copy(0, 0).start()
        x_copy(1, 1).start()
        accumulate(ma_ref, True)

    @pl.when((k > 0) & (k < k_half))
    def _():
        accumulate(ma_ref, False)

    @pl.when(k >= k_half)
    def _():
        accumulate(mb_ref, False)

    @pl.when(k == 2 * k_half - 1)
    def _():
        ss = jnp.zeros((tm, 1), F32)
        for c in range(n_chunks):
            slot = c % 2
            cols = slice(c * OUT_COLS, (c + 1) * OUT_COLS)
            x_copy(c, slot).wait()
            xn = xbuf[slot] + gate_ref[0][:, cols] * o_ref[:, cols]
            o_ref[:, cols] = xn
            ss = ss + jnp.sum(xn * xn, axis=-1, keepdims=True)
            if c + 2 < n_chunks:
                x_copy(c + 2, slot).start()
        inv = lax.rsqrt(ss / d + EPS)
        for c in range(n_chunks):
            cols = slice(c * OUT_COLS, (c + 1) * OUT_COLS)
            o_ref[:, cols] = o_ref[:, cols] * inv * gf_ref[:, cols]


def _out_proj(ma, mb, w_bf16, x2d, gate, g_final, rows_per_mod, tm=1024, tk=512):
    m, d = x2d.shape
    k_half = ma.shape[1] // tk
    tiles_per_mod = rows_per_mod // tm
    kern = functools.partial(_outproj_kernel, k_half=k_half)
    return pl.pallas_call(
        kern,
        grid=(m // tm, 2 * k_half),
        in_specs=[
            pl.BlockSpec((tm, tk), lambda i, k: (i, jnp.minimum(k, k_half - 1))),
            pl.BlockSpec((tm, tk), lambda i, k: (i, jnp.maximum(k - k_half, 0))),
            pl.BlockSpec((tk, d), lambda i, k: (k, 0)),
            pl.BlockSpec(memory_space=pl.ANY),
            pl.BlockSpec((1, 1, d), lambda i, k: (i // tiles_per_mod, 0, 0)),
            pl.BlockSpec((1, d), lambda i, k: (0, 0)),
        ],
        out_specs=pl.BlockSpec((tm, d), lambda i, k: (i, 0)),
        out_shape=jax.ShapeDtypeStruct((m, d), F32),
        scratch_shapes=[pltpu.VMEM((2, tm, OUT_COLS), F32), pltpu.SemaphoreType.DMA((2,))],
        compiler_params=pltpu.CompilerParams(
            dimension_semantics=("arbitrary", "arbitrary"), vmem_limit_bytes=VMEM_LIMIT_BYTES),
        name="out_proj",
    )(ma, mb, w_bf16, x2d, gate, g_final.reshape(1, d))


def kernel(x, c, ctx, c_ctx, w_ada, b_ada, g_norm, w_in, conv_w, conv_b, lru_lambda,
           w_rgate, b_rgate, w_igate, b_igate, w_pool, b_pool, pool_scale, w_out, g_final):
    bsz, seq_len, d = x.shape
    ctx_len = ctx.shape[1]
    depth = w_ada.shape[0]
    assert depth == 1 and d == D_MODEL
    l = 0

    cc = jnp.zeros((8, d), F32).at[:bsz].set(c).at[bsz].set(c_ctx)
    mod = _adaln(cc, w_ada[l], b_ada[l])
    shift = mod[:, None, :d]
    scale = mod[:, None, d:2 * d]
    gate = mod[:, None, 2 * d:]

    w_in_b = w_in[l].astype(BF16)
    x2d = x.reshape(bsz * seq_len, d)
    proj, w_out_b = _in_proj(x2d, g_norm[l], shift, scale, w_in_b, seq_len, 2 * MIX_W,
                             side=w_out[l])
    proj = proj.reshape(bsz, seq_len, 2 * MIX_W)
    proj_c = _in_proj(ctx.reshape(bsz * ctx_len, d), g_norm[l], shift[bsz:bsz + 1],
                      scale[bsz:bsz + 1], w_in_b, bsz * ctx_len, LRU_W,
                      ).reshape(bsz, ctx_len, LRU_W)

    wg = (0.5 * jnp.concatenate([w_rgate[l], w_igate[l]], axis=-1)).astype(BF16)
    mixed_a = _lru(proj, proj_c, conv_w[l], conv_b[l], lru_lambda[l], wg, b_rgate[l], b_igate[l])
    mixed_b = _pool(proj, w_pool[l].astype(BF16), b_pool[l], pool_scale[l])

    out = _out_proj(mixed_a.reshape(bsz * seq_len, LRU_W), mixed_b.reshape(bsz * seq_len, POOL_W),
                    w_out_b, x2d, gate, g_final, seq_len)
    return out.reshape(bsz, seq_len, d)
```

```python
import functools

import jax
import jax.numpy as jnp
import numpy as np
from jax import lax
from jax.experimental import pallas as pl
from jax.experimental.pallas import tpu as pltpu

D_MODEL = 4096
GRID_W = 64
MIX_W = 2 * D_MODEL
LRU_W = MIX_W // 2
POOL_W = MIX_W - LRU_W
LRU_HEADS = 16
LRU_HEAD_DIM = LRU_W // LRU_HEADS
CONV_W = 4
CONV_LEFT = 1
LRU_C = 8.0
POOL_WINDOWS = (2, 4, 8, 16)
POOL_GROUP_DIM = POOL_W // len(POOL_WINDOWS)
EPS = 1e-6

LANES = 128
VMEM_LIMIT_BYTES = 56 * 1024 * 1024

BF16 = jnp.bfloat16
F32 = jnp.float32


def _sigmoid(v):
    return 0.5 * jnp.tanh(0.5 * v) + 0.5


def _silu(v):
    return v * _sigmoid(v)


def _adaln_kernel(c_ref, w_ref, b_ref, o_ref):
    s = _silu(c_ref[...]).astype(BF16)
    o_ref[...] = jnp.dot(s, w_ref[...].astype(BF16), preferred_element_type=F32) + b_ref[...]


def _adaln(cc, w_ada, b_ada, tn=512):
    rows, d = cc.shape
    n = w_ada.shape[1]
    return pl.pallas_call(
        _adaln_kernel,
        grid=(n // tn,),
        in_specs=[
            pl.BlockSpec((rows, d), lambda j: (0, 0)),
            pl.BlockSpec((d, tn), lambda j: (0, j)),
            pl.BlockSpec((1, tn), lambda j: (0, j)),
        ],
        out_specs=pl.BlockSpec((rows, tn), lambda j: (0, j)),
        out_shape=jax.ShapeDtypeStruct((rows, n), F32),
        compiler_params=pltpu.CompilerParams(
            dimension_semantics=("parallel",), vmem_limit_bytes=VMEM_LIMIT_BYTES),
        name="adaln",
    )(cc, w_ada, b_ada.reshape(1, n))


NORM_ROWS = 16


def _inproj_kernel(x_ref, g_ref, shift_ref, scale_ref, w_ref, *rest, n_sub, has_side):
    if has_side:
        side_ref, o_ref, side_out_ref, h_ref, rs_ref = rest
    else:
        o_ref, h_ref, rs_ref = rest
    j = pl.program_id(1)
    sub_rows = x_ref.shape[0]

    @pl.when(j < n_sub)
    def _():
        base = pl.multiple_of(j * sub_rows, sub_rows)
        reps = x_ref.shape[1] // LANES

        def reduce_rows(r, carry):
            r0 = pl.multiple_of(r * NORM_ROWS, NORM_ROWS)
            xf = x_ref[pl.ds(r0, NORM_ROWS), :]
            ms = jnp.mean(xf * xf, axis=-1, keepdims=True)
            rs_ref[pl.ds(r0, NORM_ROWS), :] = jnp.broadcast_to(lax.rsqrt(ms + EPS),
                                                               (NORM_ROWS, LANES))
            return carry

        def scale_rows(r, carry):
            r0 = pl.multiple_of(r * NORM_ROWS, NORM_ROWS)
            rs = jnp.tile(rs_ref[pl.ds(r0, NORM_ROWS), :], (1, reps))
            y = x_ref[pl.ds(r0, NORM_ROWS), :] * rs * g_ref[...]
            h = (y * (1.0 + scale_ref[0]) + shift_ref[0]).astype(BF16)
            h_ref[pl.ds(base + r0, NORM_ROWS), :] = h
            return carry

        lax.fori_loop(0, sub_rows // NORM_ROWS, reduce_rows, 0, unroll=8)
        lax.fori_loop(0, sub_rows // NORM_ROWS, scale_rows, 0, unroll=2)

    @pl.when(j >= n_sub)
    def _():
        o_ref[...] = jnp.dot(h_ref[...], w_ref[...], preferred_element_type=F32).astype(o_ref.dtype)
        if has_side:
            side_out_ref[...] = side_ref[...].astype(BF16)


def _in_proj(x2d, g_norm, shift, scale, w_bf16, rows_per_mod, n_cols, side=None, tm=1024, tn=1024,
             sub_rows=512):
    m, d = x2d.shape
    n_sub = tm // sub_rows
    n_mm = n_cols // tn
    tiles_per_mod = rows_per_mod // tm
    kern = functools.partial(_inproj_kernel, n_sub=n_sub, has_side=side is not None)

    def mm_step(j):
        return jnp.maximum(j - n_sub, 0)

    in_specs = [
        pl.BlockSpec((sub_rows, d), lambda i, j: (i * n_sub + jnp.minimum(j, n_sub - 1), 0)),
        pl.BlockSpec((1, d), lambda i, j: (0, 0)),
        pl.BlockSpec((1, 1, d), lambda i, j: (i // tiles_per_mod, 0, 0)),
        pl.BlockSpec((1, 1, d), lambda i, j: (i // tiles_per_mod, 0, 0)),
        pl.BlockSpec((d, tn), lambda i, j: (0, mm_step(j))),
    ]
    out_specs = pl.BlockSpec((tm, tn), lambda i, j: (i, mm_step(j)))
    out_shape = jax.ShapeDtypeStruct((m, n_cols), BF16)
    args = (x2d, g_norm.reshape(1, d), shift, scale, w_bf16)
    if side is not None:
        side_rows, side_cols = side.shape
        steps = (m // tm) * n_mm
        assert side_rows % (16 * steps) == 0
        side_spec = pl.BlockSpec((side_rows // steps, side_cols),
                                 lambda i, j: (i * n_mm + mm_step(j), 0))
        in_specs.append(side_spec)
        out_specs = (out_specs, side_spec)
        out_shape = (out_shape, jax.ShapeDtypeStruct(side.shape, BF16))
        args = args + (side,)
    return pl.pallas_call(
        kern,
        grid=(m // tm, n_sub + n_mm),
        in_specs=in_specs,
        out_specs=out_specs,
        out_shape=out_shape,
        scratch_shapes=[pltpu.VMEM((tm, d), BF16), pltpu.VMEM((sub_rows, LANES), F32)],
        compiler_params=pltpu.CompilerParams(
            dimension_semantics=("parallel", "arbitrary"), vmem_limit_bytes=VMEM_LIMIT_BYTES),
        name="in_proj",
    )(*args)


LRU_TILE = 512
LRU_GROUPS = LRU_TILE // LANES
LRU_CHUNK = 256
HALO = 16
CHUNK_PITCH = LRU_CHUNK + 8
SLAB_SET = LRU_GROUPS * CHUNK_PITCH


def _lru_kernel(xa_ref, ga_ref, xc_ref, cw_ref, cb_ref, lam_ref, wg_ref, br_ref, bi_ref,
                o_ref, af0_ref, bf0_ref, ar0_ref, br0_ref, af1_ref, bf1_ref, ar1_ref, br1_ref,
                hf_ref, hr_ref, y_ref, u_ref, *, seq_len, ctx_len):
    n_chunks = seq_len // LRU_CHUNK
    y_pitch = seq_len + 8
    heads = LRU_TILE // LRU_HEAD_DIM
    coef_sets = (((af0_ref, bf0_ref), (ar0_ref, br0_ref)),
                 ((af1_ref, bf1_ref), (ar1_ref, br1_ref)))

    cw = cw_ref[...]
    cb = cb_ref[...]
    half_decay = (-0.5 * LRU_C) * jax.nn.softplus(-lam_ref[...])
    half_br = 0.5 * br_ref[...]
    half_bi = 0.5 * bi_ref[...]

    def conv(x_pad, rows):
        n = rows + 2 * HALO
        u = cb + x_pad[HALO:HALO + rows] * cw[CONV_LEFT:CONV_LEFT + 1]
        for k in range(CONV_W):
            off = k - CONV_LEFT
            if off == 0:
                continue
            u = u + pltpu.roll(x_pad, (-off) % n, axis=0)[HALO:HALO + rows] * cw[k:k + 1]
        return u

    def coefficients(u, d, rows, cset):
        a_ref, b_ref = coef_sets[cset][d]
        for hh in range(heads):
            cols = slice(hh * LRU_HEAD_DIM, (hh + 1) * LRU_HEAD_DIM)
            uh = u[:, cols]
            gates = jnp.dot(uh.astype(BF16), wg_ref[d, hh], preferred_element_type=F32)
            tr = jnp.tanh(gates[:, :LRU_HEAD_DIM] + half_br[d:d + 1, cols])
            ti = jnp.tanh(gates[:, LRU_HEAD_DIM:] + half_bi[d:d + 1, cols])
            i = 0.5 * ti + 0.5
            log_a = half_decay[d:d + 1, cols] * tr + half_decay[d:d + 1, cols]
            a = jnp.exp(log_a)
            t = jnp.tanh(log_a)
            p = -2.0 * t
            q = 1.0 - t
            root = jnp.where(p > 0.0, p * lax.rsqrt(p * q), 0.0)
            b = root * (i * uh)
            for gg in range(LRU_HEAD_DIM // LANES):
                g = hh * (LRU_HEAD_DIM // LANES) + gg
                lanes = slice(gg * LANES, (gg + 1) * LANES)
                a_ref[pl.ds(g * CHUNK_PITCH, rows), :] = a[:, lanes]
                b_ref[pl.ds(g * CHUNK_PITCH, rows), :] = b[:, lanes]

    def latent_chunk(k):
        s = pl.multiple_of(k * LRU_CHUNK, LRU_CHUNK)
        s_lo = pl.multiple_of(jnp.maximum(s - HALO, 0), HALO)
        s_hi = pl.multiple_of(jnp.minimum(s + LRU_CHUNK, seq_len - HALO), HALO)
        lo = xa_ref[0, pl.ds(s_lo, HALO), :].astype(F32)
        hi = xa_ref[0, pl.ds(s_hi, HALO), :].astype(F32)
        lo = jnp.where(k > 0, lo, 0.0)
        hi = jnp.where(k < n_chunks - 1, hi, 0.0)
        mid = xa_ref[0, pl.ds(s, LRU_CHUNK), :].astype(F32)
        return conv(jnp.concatenate([lo, mid, hi], axis=0), LRU_CHUNK)

    def load_step(refs, row):
        a_ref, b_ref = refs
        return (a_ref[pl.ds(row, LRU_GROUPS, stride=CHUNK_PITCH), :],
                b_ref[pl.ds(row, LRU_GROUPS, stride=CHUNK_PITCH), :])

    def scan_steps(cset, carry, steps, store):
        hf, hr = carry
        for j in range(steps):
            jr = steps - 1 - j
            a, b = load_step(coef_sets[cset][0], j)
            hf = a * hf + b
            a, b = load_step(coef_sets[cset][1], jr)
            hr = a * hr + b
            if store is not None:
                store(j, jr, hf, hr)
        return hf, hr

    def prepare_from_conv(k, cset):
        kr = n_chunks - 1 - k
        uf = latent_chunk(k)
        u_ref[pl.ds(pl.multiple_of(k * LRU_CHUNK, LRU_CHUNK), LRU_CHUNK), :] = uf
        coefficients(uf, 0, LRU_CHUNK, cset)
        ur = latent_chunk(kr)
        u_ref[pl.ds(pl.multiple_of(kr * LRU_CHUNK, LRU_CHUNK), LRU_CHUNK), :] = ur
        coefficients(ur, 1, LRU_CHUNK, cset)

    def prepare_from_cache(k, cset):
        kr = n_chunks - 1 - k
        coefficients(u_ref[pl.ds(pl.multiple_of(k * LRU_CHUNK, LRU_CHUNK), LRU_CHUNK), :],
                     0, LRU_CHUNK, cset)
        coefficients(u_ref[pl.ds(pl.multiple_of(kr * LRU_CHUNK, LRU_CHUNK), LRU_CHUNK), :],
                     1, LRU_CHUNK, cset)

    zpad = jnp.zeros((HALO, LRU_TILE), F32)
    u_ctx = conv(jnp.concatenate([zpad, xc_ref[0].astype(F32), zpad], axis=0), ctx_len)
    coefficients(u_ctx, 0, ctx_len, 1)
    coefficients(u_ctx, 1, ctx_len, 1)
    prepare_from_conv(jnp.int32(0), 0)
    h0 = jnp.zeros((LRU_GROUPS, LANES), F32)
    carry = scan_steps(1, (h0, h0), ctx_len, None)

    def scan_first_half(k, cset, carry):
        sf = k * LRU_CHUNK
        sr = (n_chunks - 1 - k) * LRU_CHUNK

        def store(j, jr, hf, hr):
            y_ref[pl.ds(sf + j, LRU_GROUPS, stride=y_pitch), :] = hf
            y_ref[pl.ds(sr + jr, LRU_GROUPS, stride=y_pitch), :] = hr

        return scan_steps(cset, carry, LRU_CHUNK, store)

    def scan_second_half(k, cset, carry):
        def store(j, jr, hf, hr):
            hf_ref[pl.ds(j, LRU_GROUPS, stride=CHUNK_PITCH), :] = hf
            hr_ref[pl.ds(jr, LRU_GROUPS, stride=CHUNK_PITCH), :] = hr

        carry = scan_steps(cset, carry, LRU_CHUNK, store)
        for kk, h_ref in ((k, hf_ref), (n_chunks - 1 - k, hr_ref)):
            s = pl.multiple_of(kk * LRU_CHUNK, LRU_CHUNK)
            y = jnp.concatenate(
                [y_ref[pl.ds(g * y_pitch + s, LRU_CHUNK), :]
                 + h_ref[pl.ds(g * CHUNK_PITCH, LRU_CHUNK), :] for g in range(LRU_GROUPS)],
                axis=1)
            gate = ga_ref[0, pl.ds(s, LRU_CHUNK), :].astype(F32)
            o_ref[0, pl.ds(s, LRU_CHUNK), :] = (y * _silu(gate)).astype(o_ref.dtype)
        return carry

    def first_half(m, carry):
        k = 2 * m
        prepare_from_conv(k + 1, 1)
        carry = scan_first_half(k, 0, carry)
        prepare_from_conv(k + 2, 0)
        return scan_first_half(k + 1, 1, carry)

    def second_half(m, carry):
        k = 2 * m
        prepare_from_cache(k + 1, 1)
        carry = scan_second_half(k, 0, carry)
        prepare_from_cache(jnp.minimum(k + 2, n_chunks - 1), 0)
        return scan_second_half(k + 1, 1, carry)

    quarter = n_chunks // 4
    carry = lax.fori_loop(0, quarter, first_half, carry)
    lax.fori_loop(quarter, 2 * quarter, second_half, carry)


def _lru(proj, proj_c, conv_w, conv_b, lam, wg, b_r, b_i):
    bsz, seq_len, _ = proj.shape
    ctx_len = proj_c.shape[1]
    assert seq_len % (4 * LRU_CHUNK) == 0 and ctx_len <= LRU_CHUNK and ctx_len % HALO == 0
    n_tiles = LRU_W // LRU_TILE
    heads = LRU_TILE // LRU_HEAD_DIM
    gate_block0 = (MIX_W) // LRU_TILE
    slab = pltpu.VMEM((SLAB_SET, LANES), F32)
    kern = functools.partial(_lru_kernel, seq_len=seq_len, ctx_len=ctx_len)
    return pl.pallas_call(
        kern,
        grid=(bsz, n_tiles),
        in_specs=[
            pl.BlockSpec((1, seq_len, LRU_TILE), lambda b, t: (b, 0, t)),
            pl.BlockSpec((1, seq_len, LRU_TILE), lambda b, t: (b, 0, gate_block0 + t)),
            pl.BlockSpec((1, ctx_len, LRU_TILE), lambda b, t: (b, 0, t)),
            pl.BlockSpec((CONV_W, LRU_TILE), lambda b, t: (0, t)),
            pl.BlockSpec((1, LRU_TILE), lambda b, t: (0, t)),
            pl.BlockSpec((2, LRU_TILE), lambda b, t: (0, t)),
            pl.BlockSpec((2, heads, LRU_HEAD_DIM, 2 * LRU_HEAD_DIM), lambda b, t: (0, t, 0, 0)),
            pl.BlockSpec((2, LRU_TILE), lambda b, t: (0, t)),
            pl.BlockSpec((2, LRU_TILE), lambda b, t: (0, t)),
        ],
        out_specs=pl.BlockSpec((1, seq_len, LRU_TILE), lambda b, t: (b, 0, t)),
        out_shape=jax.ShapeDtypeStruct((bsz, seq_len, LRU_W), BF16),
        scratch_shapes=[slab] * 10 + [
                        pltpu.VMEM((LRU_GROUPS * (seq_len + 8), LANES), F32),
                        pltpu.VMEM((seq_len, LRU_TILE), F32)],
        compiler_params=pltpu.CompilerParams(
            dimension_semantics=("parallel", "parallel"), vmem_limit_bytes=VMEM_LIMIT_BYTES),
        name="lru",
    )(proj, proj, proj_c, conv_w, conv_b.reshape(1, LRU_W), lam, wg, b_r, b_i)


POOL_ROWS = 1024
POOL_BAND = 256


def _pool_operators():
    t = np.arange(POOL_BAND)
    seg, pos = t // GRID_W, t % GRID_W
    wins, inv = [], []
    for w in POOL_WINDOWS:
        left = w // 2
        right = w - 1 - left
        lo = np.maximum(pos - left, 0)
        hi = np.minimum(pos + right, GRID_W - 1)
        member = ((seg[:, None] == seg[None, :])
                  & (pos[None, :] >= lo[:, None]) & (pos[None, :] <= hi[:, None]))
        wins.append(member.astype(np.float32))
        inv.append(np.broadcast_to((1.0 / (hi - lo + 1).astype(np.float32))[:, None],
                                   (POOL_BAND, LANES)))
    return jnp.asarray(np.stack(wins), BF16), jnp.asarray(np.stack(inv), F32)


def _pool_kernel(xb_ref, gb_ref, win_ref, inv_ref, wp_ref, bp_ref, sc_ref, o_ref):
    inv = jnp.tile(inv_ref[0], (1, POOL_GROUP_DIM // LANES))
    zs = []
    for sub in range(POOL_ROWS // POOL_BAND):
        xs = xb_ref[0, sub * POOL_BAND:(sub + 1) * POOL_BAND, :]
        sums = jnp.dot(win_ref[0], xs, preferred_element_type=F32)
        zs.append((sums * inv - xs.astype(F32)).astype(BF16))
    z = jnp.concatenate(zs, axis=0)
    y = (jnp.dot(z, wp_ref[0], preferred_element_type=F32) + bp_ref[...]) * sc_ref[...]
    o_ref[0] = (y * _silu(gb_ref[0].astype(F32))).astype(o_ref.dtype)


def _pool(proj, wp_bf16, b_pool, pool_scale):
    bsz, seq_len, _ = proj.shape
    assert seq_len % POOL_ROWS == 0 and POOL_BAND % GRID_W == 0
    groups = len(POOL_WINDOWS)
    xb_block0 = LRU_W // POOL_GROUP_DIM
    gb_block0 = (MIX_W + LRU_W) // POOL_GROUP_DIM
    win, inv = _pool_operators()
    return pl.pallas_call(
        _pool_kernel,
        grid=(groups, bsz, seq_len // POOL_ROWS),
        in_specs=[
            pl.BlockSpec((1, POOL_ROWS, POOL_GROUP_DIM), lambda g, b, l: (b, l, xb_block0 + g)),
            pl.BlockSpec((1, POOL_ROWS, POOL_GROUP_DIM), lambda g, b, l: (b, l, gb_block0 + g)),
            pl.BlockSpec((1, POOL_BAND, POOL_BAND), lambda g, b, l: (g, 0, 0)),
            pl.BlockSpec((1, POOL_BAND, LANES), lambda g, b, l: (g, 0, 0)),
            pl.BlockSpec((1, POOL_GROUP_DIM, POOL_GROUP_DIM), lambda g, b, l: (g, 0, 0)),
            pl.BlockSpec((1, POOL_GROUP_DIM), lambda g, b, l: (0, g)),
            pl.BlockSpec((1, POOL_GROUP_DIM), lambda g, b, l: (0, g)),
        ],
        out_specs=pl.BlockSpec((1, POOL_ROWS, POOL_GROUP_DIM), lambda g, b, l: (b, l, g)),
        out_shape=jax.ShapeDtypeStruct((bsz, seq_len, POOL_W), BF16),
        compiler_params=pltpu.CompilerParams(
            dimension_semantics=("parallel", "parallel", "parallel"),
            vmem_limit_bytes=VMEM_LIMIT_BYTES),
        name="pool",
    )(proj, proj, win, inv, wp_bf16, b_pool.reshape(1, POOL_W), pool_scale.reshape(1, POOL_W))


OUT_COLS = 512
OUT_DOT_COLS = 1024


def _outproj_kernel(ma_ref, mb_ref, w_ref, x_hbm, gate_ref, gf_ref, o_ref, xbuf, sem, *, k_half):
    i = pl.program_id(0)
    k = pl.program_id(1)
    tm, d = o_ref.shape
    n_chunks = d // OUT_COLS

    def x_copy(c, slot):
        return pltpu.make_async_copy(
            x_hbm.at[pl.ds(i * tm, tm), pl.ds(c * OUT_COLS, OUT_COLS)], xbuf.at[slot], sem.at[slot])

    def accumulate(lhs_ref, first):
        for n in range(d // OUT_DOT_COLS):
            cols = slice(n * OUT_DOT_COLS, (n + 1) * OUT_DOT_COLS)
            part = jnp.dot(lhs_ref[...], w_ref[:, cols], preferred_element_type=F32)
            if first:
                o_ref[:, cols] = part
            else:
                o_ref[:, cols] += part

    @pl.when(k == 0)
    def _():
        x_copy(0, 0).start()
        x_copy(1, 1).start()
        accumulate(ma_ref, True)

    @pl.when((k > 0) & (k < k_half))
    def _():
        accumulate(ma_ref, False)

    @pl.when(k >= k_half)
    def _():
        accumulate(mb_ref, False)

    @pl.when(k == 2 * k_half - 1)
    def _():
        ss = jnp.zeros((tm, 1), F32)
        for c in range(n_chunks):
            slot = c % 2
            cols = slice(c * OUT_COLS, (c + 1) * OUT_COLS)
            x_copy(c, slot).wait()
            xn = xbuf[slot] + gate_ref[0][:, cols] * o_ref[:, cols]
            o_ref[:, cols] = xn
            ss = ss + jnp.sum(xn * xn, axis=-1, keepdims=True)
            if c + 2 < n_chunks:
                x_copy(c + 2, slot).start()
        inv = lax.rsqrt(ss / d + EPS)
        for c in range(n_chunks):
            cols = slice(c * OUT_COLS, (c + 1) * OUT_COLS)
            o_ref[:, cols] = o_ref[:, cols] * inv * gf_ref[:, cols]


def _out_proj(ma, mb, w_bf16, x2d, gate, g_final, rows_per_mod, tm=1024, tk=512):
    m, d = x2d.shape
    k_half = ma.shape[1] // tk
    tiles_per_mod = rows_per_mod // tm
    kern = functools.partial(_outproj_kernel, k_half=k_half)
    return pl.pallas_call(
        kern,
        grid=(m // tm, 2 * k_half),
        in_specs=[
            pl.BlockSpec((tm, tk), lambda i, k: (i, jnp.minimum(k, k_half - 1))),
            pl.BlockSpec((tm, tk), lambda i, k: (i, jnp.maximum(k - k_half, 0))),
            pl.BlockSpec((tk, d), lambda i, k: (k, 0)),
            pl.BlockSpec(memory_space=pl.ANY),
            pl.BlockSpec((1, 1, d), lambda i, k: (i // tiles_per_mod, 0, 0)),
            pl.BlockSpec((1, d), lambda i, k: (0, 0)),
        ],
        out_specs=pl.BlockSpec((tm, d), lambda i, k: (i, 0)),
        out_shape=jax.ShapeDtypeStruct((m, d), F32),
        scratch_shapes=[pltpu.VMEM((2, tm, OUT_COLS), F32), pltpu.SemaphoreType.DMA((2,))],
        compiler_params=pltpu.CompilerParams(
            dimension_semantics=("arbitrary", "arbitrary"), vmem_limit_bytes=VMEM_LIMIT_BYTES),
        name="out_proj",
    )(ma, mb, w_bf16, x2d, gate, g_final.reshape(1, d))


def kernel(x, c, ctx, c_ctx, w_ada, b_ada, g_norm, w_in, conv_w, conv_b, lru_lambda,
           w_rgate, b_rgate, w_igate, b_igate, w_pool, b_pool, pool_scale, w_out, g_final):
    bsz, seq_len, d = x.shape
    ctx_len = ctx.shape[1]
    depth = w_ada.shape[0]
    assert depth == 1 and d == D_MODEL
    l = 0

    cc = jnp.zeros((8, d), F32).at[:bsz].set(c).at[bsz].set(c_ctx)
    mod = _adaln(cc, w_ada[l], b_ada[l])
    shift = mod[:, None, :d]
    scale = mod[:, None, d:2 * d]
    gate = mod[:, None, 2 * d:]

    w_in_b = w_in[l].astype(BF16)
    x2d = x.reshape(bsz * seq_len, d)
    proj, w_out_b = _in_proj(x2d, g_norm[l], shift, scale, w_in_b, seq_len, 2 * MIX_W,
                             side=w_out[l])
    proj = proj.reshape(bsz, seq_len, 2 * MIX_W)
    proj_c = _in_proj(ctx.reshape(bsz * ctx_len, d), g_norm[l], shift[bsz:bsz + 1],
                      scale[bsz:bsz + 1], w_in_b, bsz * ctx_len, LRU_W,
                      ).reshape(bsz, ctx_len, LRU_W)

    wg = (0.5 * jnp.concatenate([w_rgate[l], w_igate[l]], axis=-1)).astype(BF16)
    mixed_a = _lru(proj, proj_c, conv_w[l], conv_b[l], lru_lambda[l], wg, b_rgate[l], b_igate[l])
    mixed_b = _pool(proj, w_pool[l].astype(BF16), b_pool[l], pool_scale[l])

    out = _out_proj(mixed_a.reshape(bsz * seq_len, LRU_W), mixed_b.reshape(bsz * seq_len, POOL_W),
                    w_out_b, x2d, gate, g_final, seq_len)
    return out.reshape(bsz, seq_len, d)
```
